```python
import math
import jax, jax.numpy as jnp
from jax import lax
import numpy as np

D_MODEL = 2048
BATCH = 2
SEQ = 4096
DEPTH = 4

A_HEADS = 4
A_QK_DIM = 64
A_V_DIM = 2 * A_QK_DIM
Q_BLOCK = 128
ROPE_THETA = 10000.0
B_HEADS = 6
B_DK = 128
B_DV = 128
B_CHUNK = 16
C_HEADS = 6
C_DK = 128
C_DV = 128
CONV_K = 4
C_CHUNK = 64
D_MIX = A_HEADS * A_V_DIM + B_HEADS * B_DV + C_HEADS * C_DV
IN_SPLITS = (
    A_HEADS * 2 * A_QK_DIM, A_HEADS * 2 * A_QK_DIM, A_HEADS * A_V_DIM,
    B_HEADS * B_DK, B_HEADS * B_DK, B_HEADS * B_DV, B_HEADS * B_DV,
    C_HEADS * C_DK, C_HEADS * C_DK, C_HEADS * C_DV, C_HEADS * C_DV,
    C_HEADS, C_HEADS,
)
D_IN = sum(IN_SPLITS)
D_FF = 5632
EPS = 1e-6

kernel_name = "hymba_style_diffattn_hgrn2_gdn_macaron"

F32 = jnp.float32


def rmsnorm(x, w):
    xf = x.astype(F32)
    y = xf * lax.rsqrt(jnp.mean(xf * xf, axis=-1, keepdims=True) + EPS)
    return (y * w.astype(F32)).astype(x.dtype)


def l2norm(x):
    return x * lax.rsqrt(jnp.sum(x * x, axis=-1, keepdims=True) + EPS)


def swiglu(h, w_gate, w_up, w_down):
    return (jax.nn.silu(h @ w_gate) * (h @ w_up)) @ w_down


def split_heads(t, n_heads):
    b, s, c = t.shape
    return t.reshape(b, s, n_heads, c // n_heads).transpose(0, 2, 1, 3)


def merge_heads(t):
    b, h, s, d = t.shape
    return t.transpose(0, 2, 1, 3).reshape(b, s, h * d)


def rope(t, pos):
    half = t.shape[-1] // 2
    inv_freq = 1.0 / (ROPE_THETA ** (jnp.arange(half, dtype=F32) / half))
    ang = pos.astype(F32)[:, None] * inv_freq[None, :]
    cos, sin = jnp.cos(ang), jnp.sin(ang)
    t1, t2 = t[..., :half], t[..., half:]
    return jnp.concatenate([t1 * cos - t2 * sin, t2 * cos + t1 * sin], axis=-1)


def causal_depthwise_conv(x, w):
    c = x.shape[-1]
    return lax.conv_general_dilated(
        x, w[:, None, :], window_strides=(1,), padding=[(CONV_K - 1, 0)],
        dimension_numbers=("NWC", "WIO", "NWC"), feature_group_count=c)


def diff_attention(q_in, k_in, v_in, lq1, lk1, lq2, lk2, gain, lambda_init):
    bn, s, _ = q_in.shape
    q = q_in.astype(F32).reshape(bn, s, A_HEADS, 2, A_QK_DIM).transpose(0, 2, 3, 1, 4)
    k = k_in.astype(F32).reshape(bn, s, A_HEADS, 2, A_QK_DIM).transpose(0, 2, 3, 1, 4)
    v = split_heads(v_in.astype(F32), A_HEADS)
    pos = jnp.arange(s)
    q = rope(q, pos) * (A_QK_DIM ** -0.5)
    k = rope(k, pos)
    lam = (jnp.exp(jnp.sum(lq1.astype(F32) * lk1.astype(F32)))
           - jnp.exp(jnp.sum(lq2.astype(F32) * lk2.astype(F32))) + lambda_init)
    outs = []
    for blk in range(s // Q_BLOCK):
        q0 = blk * Q_BLOCK
        kend = q0 + Q_BLOCK
        scores = jnp.einsum("bhcqd,bhckd->bhcqk", q[:, :, :, q0:kend], k[:, :, :, :kend])
        mask = (q0 + jnp.arange(Q_BLOCK))[:, None] >= jnp.arange(kend)[None, :]
        p = jax.nn.softmax(jnp.where(mask, scores, -jnp.inf), axis=-1)
        w = p[:, :, 0] - lam * p[:, :, 1]
        outs.append(jnp.einsum("bhqk,bhkd->bhqd", w, v[:, :, :kend]))
    o = jnp.concatenate(outs, axis=2)
    o = rmsnorm(o, gain) * (1.0 - lambda_init)
    return merge_heads(o)


def hgrn2(q_in, f_in, i_in, g_in, lower_bound, gain):
    bn, s, _ = q_in.shape
    q = jax.nn.silu(split_heads(q_in.astype(F32), B_HEADS))
    lb = lower_bound.astype(F32).reshape(B_HEADS, 1, B_DK)
    f = lb + (1.0 - lb) * jax.nn.sigmoid(split_heads(f_in.astype(F32), B_HEADS))
    log_f = jnp.log(f)
    k = 1.0 - f
    v = split_heads(i_in.astype(F32), B_HEADS)
    n = s // B_CHUNK

    def to_chunks(t):
        return jnp.moveaxis(t.reshape(bn, B_HEADS, n, B_CHUNK, t.shape[-1]), 2, 0)

    qc, kc, vc = to_chunks(q), to_chunks(k), to_chunks(v)
    bc = jnp.cumsum(to_chunks(log_f), axis=-2)
    causal = jnp.tril(jnp.ones((B_CHUNK, B_CHUNK), dtype=bool))[:, :, None]

    def step(state, inp):
        qn, kn, vn, bnk = inp
        b_last = bnk[:, :, -1:, :]
        diff = bnk[:, :, :, None, :] - bnk[:, :, None, :, :]
        decay = jnp.exp(jnp.where(causal, diff, -jnp.inf))
        attn = jnp.einsum("bhtd,bhsd,bhtsd->bhts", qn, kn, decay)
        o = (jnp.einsum("bhts,bhse->bhte", attn, vn)
             + jnp.einsum("bhtd,bhde->bhte", qn * jnp.exp(bnk), state))
        new_state = (jnp.exp(b_last)[:, :, 0, :, None] * state
                     + jnp.einsum("bhsd,bhse->bhde", kn * jnp.exp(b_last - bnk), vn))
        return new_state, o

    state0 = jnp.zeros((bn, B_HEADS, B_DK, B_DV), F32)
    _, o = lax.scan(step, state0, (qc, kc, vc, bc))
    o = jnp.moveaxis(o, 0, 2).reshape(bn, B_HEADS, s, B_DV)
    o = rmsnorm(o, gain) * jax.nn.silu(split_heads(g_in.astype(F32), B_HEADS))
    return merge_heads(o)


def gated_deltanet(q_in, k_in, v_in, z_in, beta_in, a_in, conv_w, a_log, dt_bias, gain):
    bn, s, _ = q_in.shape
    qkv = jnp.concatenate([q_in, k_in, v_in], axis=-1).astype(F32)
    qkv = jax.nn.silu(causal_depthwise_conv(qkv, conv_w.astype(F32)))
    q, k, v = jnp.split(qkv, [C_HEADS * C_DK, 2 * C_HEADS * C_DK], axis=-1)
    q = l2norm(split_heads(q, C_HEADS)) * (C_DK ** -0.5)
    k = l2norm(split_heads(k, C_HEADS))
    v = split_heads(v, C_HEADS)
    beta = jax.nn.sigmoid(beta_in.astype(F32)).transpose(0, 2, 1)
    g = (-jnp.exp(a_log.astype(F32))[None, :, None]
         * jax.nn.softplus(a_in.astype(F32).transpose(0, 2, 1) + dt_bias.astype(F32)[None, :, None]))
    n = s // C_CHUNK
    qc = q.reshape(bn, C_HEADS, n, C_CHUNK, C_DK)
    kc = k.reshape(bn, C_HEADS, n, C_CHUNK, C_DK)
    vc = v.reshape(bn, C_HEADS, n, C_CHUNK, C_DV)
    betac = beta.reshape(bn, C_HEADS, n, C_CHUNK)
    bc = jnp.cumsum(g.reshape(bn, C_HEADS, n, C_CHUNK), axis=-1)

    incl = jnp.tril(jnp.ones((C_CHUNK, C_CHUNK), dtype=bool))
    strict = jnp.tril(jnp.ones((C_CHUNK, C_CHUNK), dtype=bool), -1)
    decay = jnp.exp(jnp.where(incl, bc[..., :, None] - bc[..., None, :], -jnp.inf))
    k_beta = kc * betac[..., None]
    v_beta = vc * betac[..., None]
    l_mat = jnp.where(strict, jnp.einsum("bhnid,bhnjd->bhnij", k_beta, kc) * decay, 0.0)
    eye = jnp.eye(C_CHUNK, dtype=F32)
    rhs = jnp.concatenate([v_beta, k_beta * jnp.exp(bc)[..., None]], axis=-1)
    sol = lax.linalg.triangular_solve(eye + l_mat, rhs, left_side=True, lower=True,
                                      unit_diagonal=True)
    u, w = sol[..., :C_DV], sol[..., C_DV:]
    attn = jnp.einsum("bhnid,bhnjd->bhnij", qc, kc) * decay

    def mv(t):
        return jnp.moveaxis(t, 2, 0)

    def step(state, inp):
        qn, kn, un, wn, bnk, an = inp
        v_new = un - jnp.einsum("bhld,bhde->bhle", wn, state)
        o = (jnp.einsum("bhld,bhde->bhle", qn * jnp.exp(bnk)[..., None], state)
             + jnp.einsum("bhij,bhje->bhie", an, v_new))
        b_last = bnk[..., -1]
        new_state = (state * jnp.exp(b_last)[..., None, None]
                     + jnp.einsum("bhld,bhle->bhde", kn * jnp.exp(b_last[..., None] - bnk)[..., None], v_new))
        return new_state, o

    state0 = jnp.zeros((bn, C_HEADS, C_DK, C_DV), F32)
    _, o = lax.scan(step, state0, (mv(qc), mv(kc), mv(u), mv(w), mv(bc), mv(attn)))
    o = jnp.moveaxis(o, 0, 2).reshape(bn, C_HEADS, s, C_DV)
    o = rmsnorm(o, gain) * jax.nn.silu(split_heads(z_in.astype(F32), C_HEADS))
    return merge_heads(o)


def setup_inputs(seed: int = 0) -> dict:
    key = jax.random.key(seed)
    ks = iter(jax.random.split(key, 32))

    def nrm(shape, scale):
        return jax.random.normal(next(ks), shape, F32) * scale

    def gain(shape):
        return 1.0 + nrm(shape, 0.02)

    return {
        "x": nrm((BATCH, SEQ, D_MODEL), 1.0),
        "ffn1_norm": gain((DEPTH, D_MODEL)),
        "ffn1_w_gate": nrm((DEPTH, D_MODEL, D_FF), D_MODEL ** -0.5),
        "ffn1_w_up": nrm((DEPTH, D_MODEL, D_FF), D_MODEL ** -0.5),
        "ffn1_w_down": nrm((DEPTH, D_FF, D_MODEL), D_FF ** -0.5),
        "mix_norm": gain((DEPTH, D_MODEL)),
        "w_in": nrm((DEPTH, D_MODEL, D_IN), D_MODEL ** -0.5),
        "w_out": nrm((DEPTH, D_MIX, D_MODEL), D_MIX ** -0.5),
        "lambda_q1": nrm((DEPTH, A_QK_DIM), 0.1),
        "lambda_k1": nrm((DEPTH, A_QK_DIM), 0.1),
        "lambda_q2": nrm((DEPTH, A_QK_DIM), 0.1),
        "lambda_k2": nrm((DEPTH, A_QK_DIM), 0.1),
        "diff_gain": gain((DEPTH, A_V_DIM)),
        "hgrn_lb_param": nrm((DEPTH, B_HEADS * B_DK), 0.1),
        "hgrn_gain": gain((DEPTH, B_DV)),
        "gdn_conv_w": nrm((DEPTH, CONV_K, C_HEADS * (2 * C_DK + C_DV)), CONV_K ** -0.5),
        "gdn_a_log": jnp.log(jax.random.uniform(next(ks), (DEPTH, C_HEADS), F32, 1.0, 16.0)),
        "gdn_dt_bias": nrm((DEPTH, C_HEADS), 0.1),
        "gdn_gain": gain((DEPTH, C_DV)),
        "ffn2_norm": gain((DEPTH, D_MODEL)),
        "ffn2_w_gate": nrm((DEPTH, D_MODEL, D_FF), D_MODEL ** -0.5),
        "ffn2_w_up": nrm((DEPTH, D_MODEL, D_FF), D_MODEL ** -0.5),
        "ffn2_w_down": nrm((DEPTH, D_FF, D_MODEL), D_FF ** -0.5),
        "final_norm": gain((D_MODEL,)),
    }


def reference(x, ffn1_norm, ffn1_w_gate, ffn1_w_up, ffn1_w_down, mix_norm, w_in, w_out,
              lambda_q1, lambda_k1, lambda_q2, lambda_k2, diff_gain,
              hgrn_lb_param, hgrn_gain, gdn_conv_w, gdn_a_log, gdn_dt_bias, gdn_gain,
              ffn2_norm, ffn2_w_gate, ffn2_w_up, ffn2_w_down, final_norm):
    lbs = jax.nn.softmax(hgrn_lb_param.astype(F32), axis=0)
    lbs = jnp.cumsum(lbs, axis=0) - lbs[0]
    split_points = []
    acc = 0
    for sz in IN_SPLITS[:-1]:
        acc += sz
        split_points.append(acc)

    for l in range(DEPTH):
        h = rmsnorm(x, ffn1_norm[l])
        x = x + 0.5 * swiglu(h, ffn1_w_gate[l], ffn1_w_up[l], ffn1_w_down[l])
        h = rmsnorm(x, mix_norm[l])
        (aq, ak, av, bq, bf, bi, bg, cq, ck, cv, cz, cb, ca) = jnp.split(h @ w_in[l], split_points, axis=-1)
        lambda_init = 0.8 - 0.6 * math.exp(-0.3 * l)
        oa = diff_attention(aq, ak, av, lambda_q1[l], lambda_k1[l], lambda_q2[l], lambda_k2[l],
                            diff_gain[l], lambda_init)
        ob = hgrn2(bq, bf, bi, bg, lbs[l], hgrn_gain[l])
        oc = gated_deltanet(cq, ck, cv, cz, cb, ca, gdn_conv_w[l], gdn_a_log[l], gdn_dt_bias[l],
                            gdn_gain[l])
        mixed = jnp.concatenate([oa, ob, oc], axis=-1).astype(x.dtype)
        x = x + mixed @ w_out[l]
        h = rmsnorm(x, ffn2_norm[l])
        x = x + 0.5 * swiglu(h, ffn2_w_gate[l], ffn2_w_up[l], ffn2_w_down[l])
    return rmsnorm(x, final_norm)
```

```python
import functools
import math

import jax
import jax.numpy as jnp
from jax import lax
from jax.experimental import pallas as pl
from jax.experimental.pallas import tpu as pltpu

F32 = jnp.float32
BF16 = jnp.bfloat16
HIGHEST = lax.Precision.HIGHEST

D_MODEL = 2048
DEPTH = 4
A_HEADS = 4
A_QK_DIM = 64
A_V_DIM = 128
ROPE_THETA = 10000.0
B_HEADS = 6
C_HEADS = 6
HEAD_W = 128
CONV_K = 4
D_FF = 5632
EPS = 1e-6
A_W = A_HEADS * HEAD_W
B_W = B_HEADS * HEAD_W
C_W = C_HEADS * HEAD_W
P_MAIN = 3 * A_W + 4 * B_W + 4 * C_W
P_TAIL = 2 * C_HEADS

LANES = 128
SUBLANES = 8
VMEM_LIMIT_BYTES = 56 * 1024 * 1024

FFN_TM = 1024
FFN_TF = 256
INPROJ_TM = 1024
INPROJ_TN = 768
OUTPROJ_TM = 1024
OUTPROJ_TK = 256
PREP_TS = 512
ATT_T = 512
HGRN_C = 128
HGRN_SUB = 16
GDN_C = 64
GDN_NB = 4
GDN_CB = GDN_C * GDN_NB


def _cparams(sem):
    return pltpu.CompilerParams(dimension_semantics=sem, vmem_limit_bytes=VMEM_LIMIT_BYTES)


def _rms(x, w):
    return x * lax.rsqrt(jnp.mean(x * x, axis=-1, keepdims=True) + EPS) * w


def _silu(x):
    return x * jax.nn.sigmoid(x)


def _dot(a, b):
    return jnp.dot(a.astype(BF16), b.astype(BF16), preferred_element_type=F32)


def _dot_nt(a, b):
    return lax.dot_general(a.astype(BF16), b.astype(BF16), (((1,), (1,)), ((), ())),
                           preferred_element_type=F32)


def _dot_tn(a, b):
    return lax.dot_general(a.astype(BF16), b.astype(BF16), (((0,), (0,)), ((), ())),
                           preferred_element_type=F32)


def _dot_hi(a, b):
    return jnp.dot(a, b, preferred_element_type=F32, precision=HIGHEST)


def _ffn_kernel(x_ref, nw_ref, wg_ref, wu_ref, wd_ref, fw_ref, o_ref, h_ref, *, final_norm):
    j = pl.program_id(1)

    @pl.when(j == 0)
    def _():
        x = x_ref[...]
        h_ref[...] = _rms(x, nw_ref[...]).astype(BF16)
        o_ref[...] = x

    h = h_ref[...]
    g = jnp.dot(h, wg_ref[...].astype(BF16), preferred_element_type=F32)
    u = jnp.dot(h, wu_ref[...].astype(BF16), preferred_element_type=F32)
    a = (0.5 * _silu(g) * u).astype(BF16)
    o_ref[...] += jnp.dot(a, wd_ref[...].astype(BF16), preferred_element_type=F32)

    if final_norm:
        @pl.when(j == pl.num_programs(1) - 1)
        def _():
            o_ref[...] = _rms(o_ref[...], fw_ref[...])


def _ffn(x, norm_w, w_gate, w_up, w_down, final_w, layer, final_norm):
    t = x.shape[0]
    tm = min(FFN_TM, t)
    grid = (t // tm, D_FF // FFN_TF)
    return pl.pallas_call(
        functools.partial(_ffn_kernel, final_norm=final_norm),
        grid=grid,
        in_specs=[
            pl.BlockSpec((tm, D_MODEL), lambda i, j: (i, 0)),
            pl.BlockSpec((None, 1, D_MODEL), lambda i, j: (layer, 0, 0)),
            pl.BlockSpec((None, D_MODEL, FFN_TF), lambda i, j: (layer, 0, j)),
            pl.BlockSpec((None, D_MODEL, FFN_TF), lambda i, j: (layer, 0, j)),
            pl.BlockSpec((None, FFN_TF, D_MODEL), lambda i, j: (layer, j, 0)),
            pl.BlockSpec((1, D_MODEL), lambda i, j: (0, 0)),
        ],
        out_specs=pl.BlockSpec((tm, D_MODEL), lambda i, j: (i, 0)),
        out_shape=jax.ShapeDtypeStruct((t, D_MODEL), F32),
        scratch_shapes=[pltpu.VMEM((tm, D_MODEL), BF16)],
        compiler_params=_cparams(("parallel", "arbitrary")),
        name="ffn",
    )(x, norm_w, w_gate, w_up, w_down, final_w)


def _inproj_kernel(x_ref, nw_ref, w_ref, wt_ref, p_ref, pt_ref, h_ref):
    j = pl.program_id(1)

    @pl.when(j == 0)
    def _():
        h = _rms(x_ref[...], nw_ref[...]).astype(BF16)
        h_ref[...] = h
        pt_ref[...] = jnp.dot(h, wt_ref[...].astype(BF16), preferred_element_type=F32)

    p_ref[...] = jnp.dot(h_ref[...], w_ref[...].astype(BF16), preferred_element_type=F32)


def _inproj(x, norm_w, w_in, w_tail, layer):
    t = x.shape[0]
    tm = min(INPROJ_TM, t)
    grid = (t // tm, P_MAIN // INPROJ_TN)
    return pl.pallas_call(
        _inproj_kernel,
        grid=grid,
        in_specs=[
            pl.BlockSpec((tm, D_MODEL), lambda i, j: (i, 0)),
            pl.BlockSpec((None, 1, D_MODEL), lambda i, j: (layer, 0, 0)),
            pl.BlockSpec((None, D_MODEL, INPROJ_TN), lambda i, j: (layer, 0, j)),
            pl.BlockSpec((None, D_MODEL, LANES), lambda i, j: (layer, 0, 0)),
        ],
        out_specs=[
            pl.BlockSpec((tm, INPROJ_TN), lambda i, j: (i, j)),
            pl.BlockSpec((tm, LANES), lambda i, j: (i, 0)),
        ],
        out_shape=[jax.ShapeDtypeStruct((t, P_MAIN), F32),
                   jax.ShapeDtypeStruct((t, LANES), F32)],
        scratch_shapes=[pltpu.VMEM((tm, D_MODEL), BF16)],
        compiler_params=_cparams(("parallel", "arbitrary")),
        name="inproj",
    )(x, norm_w, w_in, w_tail)


_A_KB = A_W // OUTPROJ_TK
_B_KB = B_W // OUTPROJ_TK
_C_KB = C_W // OUTPROJ_TK


def _outproj_kernel(x_ref, oa_ref, ob_ref, oc_ref, w_ref, o_ref):
    k = pl.program_id(1)
    w = w_ref[...].astype(BF16)

    @pl.when(k == 0)
    def _():
        o_ref[...] = x_ref[...]

    @pl.when(k < _A_KB)
    def _():
        o_ref[...] += jnp.dot(oa_ref[...], w, preferred_element_type=F32)

    @pl.when(jnp.logical_and(k >= _A_KB, k < _A_KB + _B_KB))
    def _():
        o_ref[...] += jnp.dot(ob_ref[...], w, preferred_element_type=F32)

    @pl.when(k >= _A_KB + _B_KB)
    def _():
        o_ref[...] += jnp.dot(oc_ref[...], w, preferred_element_type=F32)


def _outproj(x, oa, ob, oc, w_out, layer):
    t = x.shape[0]
    tm = min(OUTPROJ_TM, t)
    grid = (t // tm, D_MODEL // OUTPROJ_TK)
    return pl.pallas_call(
        _outproj_kernel,
        grid=grid,
        in_specs=[
            pl.BlockSpec((tm, D_MODEL), lambda i, k: (i, 0)),
            pl.BlockSpec((tm, OUTPROJ_TK), lambda i, k: (i, jnp.clip(k, 0, _A_KB - 1))),
            pl.BlockSpec((tm, OUTPROJ_TK), lambda i, k: (i, jnp.clip(k - _A_KB, 0, _B_KB - 1))),
            pl.BlockSpec((tm, OUTPROJ_TK),
                         lambda i, k: (i, jnp.clip(k - _A_KB - _B_KB, 0, _C_KB - 1))),
            pl.BlockSpec((None, OUTPROJ_TK, D_MODEL), lambda i, k: (layer, k, 0)),
        ],
        out_specs=pl.BlockSpec((tm, D_MODEL), lambda i, k: (i, 0)),
        out_shape=jax.ShapeDtypeStruct((t, D_MODEL), F32),
        compiler_params=_cparams(("parallel", "arbitrary")),
        name="outproj",
    )(x, oa, ob, oc, w_out)


def _rope_tables(s):
    half = A_QK_DIM // 2
    inv_freq = 1.0 / (ROPE_THETA ** (jnp.arange(half, dtype=F32) / half))
    ang = jnp.arange(s).astype(F32)[:, None] * inv_freq[None, :]
    cos, sin = jnp.cos(ang), jnp.sin(ang)
    cos_t = jnp.concatenate([cos, cos, cos, cos], axis=-1)
    sin_t = jnp.concatenate([-sin, sin, -sin, sin], axis=-1)
    return cos_t, sin_t


def _attn_prep_kernel(q_ref, k_ref, v_ref, cos_ref, sin_ref, qo_ref, k0_ref, k1_ref, vo_ref):
    cos = cos_ref[...]
    sin = sin_ref[...]
    lane = lax.broadcasted_iota(jnp.int32, cos.shape, 1)
    first_half = (lane & (A_QK_DIM - 1)) < (A_QK_DIM // 2)
    comp0 = lane < A_QK_DIM
    half = A_QK_DIM // 2

    def rope(t):
        swapped = jnp.where(first_half, pltpu.roll(t, LANES - half, 1), pltpu.roll(t, half, 1))
        return t * cos + swapped * sin

    for h in range(A_HEADS):
        sl = slice(h * HEAD_W, (h + 1) * HEAD_W)
        qo_ref[:, sl] = (rope(q_ref[:, sl]) * (A_QK_DIM ** -0.5)).astype(BF16)
        kr = rope(k_ref[:, sl])
        k0_ref[:, sl] = jnp.where(comp0, kr, 0.0).astype(BF16)
        k1_ref[:, sl] = jnp.where(comp0, 0.0, kr).astype(BF16)
    vo_ref[...] = v_ref[...].astype(BF16)


def _attn_prep(p, cos_t, sin_t, s):
    t = p.shape[0]
    ts = min(PREP_TS, s)
    ns = s // ts
    blk = lambda c: pl.BlockSpec((ts, A_W), lambda i: (i, c))
    tab = pl.BlockSpec((ts, LANES), lambda i: (i % ns, 0))
    out = jax.ShapeDtypeStruct((t, A_W), BF16)
    return pl.pallas_call(
        _attn_prep_kernel,
        grid=(t // ts,),
        in_specs=[blk(0), blk(1), blk(2), tab, tab],
        out_specs=[pl.BlockSpec((ts, A_W), lambda i: (i, 0))] * 4,
        out_shape=[out] * 4,
        compiler_params=_cparams(("parallel",)),
        name="attn_prep",
    )(p, p, p, cos_t, sin_t)


def _attn_kernel(q_ref, k0_ref, k1_ref, v_ref, lam_ref, gain_ref, o_ref,
                 m_ref, l_ref, acc_ref, *, lambda_init):
    qi = pl.program_id(2)
    ki = pl.program_id(3)

    @pl.when(ki == 0)
    def _():
        m_ref[...] = jnp.full(m_ref.shape, -jnp.inf, F32)
        l_ref[...] = jnp.zeros(l_ref.shape, F32)
        acc_ref[...] = jnp.zeros(acc_ref.shape, F32)

    def step(masked):
        q = q_ref[...]
        v = v_ref[...]
        for c, k_ref in enumerate((k0_ref, k1_ref)):
            s = lax.dot_general(q, k_ref[...], (((1,), (1,)), ((), ())),
                                preferred_element_type=F32)
            if masked:
                row = lax.broadcasted_iota(jnp.int32, s.shape, 0)
                col = lax.broadcasted_iota(jnp.int32, s.shape, 1)
                s = jnp.where(row >= col, s, -jnp.inf)
            m_old = m_ref[c]
            m_new = jnp.maximum(m_old, jnp.max(s, axis=-1, keepdims=True))
            alpha = jnp.exp(m_old - m_new)
            p = jnp.exp(s - m_new)
            l_ref[c] = alpha * l_ref[c] + jnp.sum(p, axis=-1, keepdims=True)
            acc_ref[c] = alpha * acc_ref[c] + jnp.dot(p.astype(BF16), v,
                                                      preferred_element_type=F32)
            m_ref[c] = m_new

    @pl.when(ki < qi)
    def _():
        step(False)

    @pl.when(ki == qi)
    def _():
        step(True)
        lp = lam_ref[...]
        lam = (jnp.exp(jnp.sum(lp[0:1] * lp[1:2], axis=-1, keepdims=True))
               - jnp.exp(jnp.sum(lp[2:3] * lp[3:4], axis=-1, keepdims=True)) + lambda_init)
        o = acc_ref[0] / l_ref[0] - lam * (acc_ref[1] / l_ref[1])
        o_ref[...] = (_rms(o, gain_ref[...]) * (1.0 - lambda_init)).astype(BF16)


def _attn(qr, k0, k1, vb, lam_p, gain, layer, b, s):
    tq = min(ATT_T, s)
    nq = s // tq
    lambda_init = 0.8 - 0.6 * math.exp(-0.3 * layer)
    qmap = lambda bi, h, qi, ki: (bi * nq + qi, h)
    kmap = lambda bi, h, qi, ki: (bi * nq + jnp.minimum(ki, qi), h)
    return pl.pallas_call(
        functools.partial(_attn_kernel, lambda_init=lambda_init),
        grid=(b, A_HEADS, nq, nq),
        in_specs=[
            pl.BlockSpec((tq, HEAD_W), qmap),
            pl.BlockSpec((tq, HEAD_W), kmap),
            pl.BlockSpec((tq, HEAD_W), kmap),
            pl.BlockSpec((tq, HEAD_W), kmap),
            pl.BlockSpec((None, 4, A_QK_DIM), lambda bi, h, qi, ki: (layer, 0, 0)),
            pl.BlockSpec((None, 1, A_V_DIM), lambda bi, h, qi, ki: (layer, 0, 0)),
        ],
        out_specs=pl.BlockSpec((tq, HEAD_W), qmap),
        out_shape=jax.ShapeDtypeStruct((b * s, A_W), BF16),
        scratch_shapes=[pltpu.VMEM((2, tq, 1), F32), pltpu.VMEM((2, tq, 1), F32),
                        pltpu.VMEM((2, tq, A_V_DIM), F32)],
        compiler_params=_cparams(("parallel", "parallel", "parallel", "arbitrary")),
        name="diff_attn",
    )(qr, k0, k1, vb, lam_p, gain)


def _group_row(x, group, row):
    n = x.shape[0]
    xg = x.reshape(n // group, group, x.shape[1])
    return jnp.broadcast_to(xg[:, row:row + 1, :], xg.shape).reshape(x.shape)


def _hgrn_kernel(q_ref, f_ref, i_ref, g_ref, lbp_ref, gain_ref, o_ref, st_ref, *, layer):
    c = pl.program_id(2)
    n = HGRN_C

    @pl.when(c == 0)
    def _():
        st_ref[...] = jnp.zeros(st_ref.shape, F32)

    lp = lbp_ref[...]
    e = jnp.exp(lp - jnp.max(lp, axis=0, keepdims=True))
    sm = e / jnp.sum(e, axis=0, keepdims=True)
    lb = jnp.zeros((1, HEAD_W), F32)
    for r in range(1, layer + 1):
        lb = lb + sm[r:r + 1]

    fg = lb + (1.0 - lb) * jax.nn.sigmoid(f_ref[...])
    kk = 1.0 - fg
    qs = _silu(q_ref[...])
    v = i_ref[...]

    row = lax.broadcasted_iota(jnp.int32, (n, n), 0)
    col = lax.broadcasted_iota(jnp.int32, (n, n), 1)
    tok = lax.broadcasted_iota(jnp.int32, (n, HEAD_W), 0)
    tril = (row >= col).astype(F32)
    b = _dot_hi(tril, jnp.log(fg))
    b_last = b[n - 1:n, :]

    st = st_ref[...]
    o = _dot_nt(qs * jnp.exp(b), st)

    a = jnp.zeros((n, n), F32)
    h = n // 2
    while h >= HGRN_SUB:
        upper = (tok & h) != 0
        bref = _group_row(b, 2 * h, h - 1)
        qh = jnp.where(upper, qs * jnp.exp(jnp.where(upper, b - bref, 0.0)), 0.0)
        kh = jnp.where(upper, 0.0, kk * jnp.exp(jnp.where(upper, 0.0, bref - b)))
        same = (row & -(2 * h)) == (col & -(2 * h))
        a = a + jnp.where(same, _dot_nt(qh, kh), 0.0)
        h //= 2
    o = o + _dot(a, v)

    tmod = tok & (HGRN_SUB - 1)
    for j in range(HGRN_SUB):
        valid = tmod >= j
        kj = _group_row(kk, HGRN_SUB, j)
        bj = _group_row(b, HGRN_SUB, j)
        vj = _group_row(v, HGRN_SUB, j)
        z = jnp.where(valid, qs * kj * jnp.exp(jnp.where(valid, b - bj, 0.0)), 0.0)
        o = o + jnp.sum(z, axis=-1, keepdims=True) * vj

    o_ref[...] = (_rms(o, gain_ref[...]) * _silu(g_ref[...])).astype(BF16)

    kd = kk * jnp.exp(b_last - b)
    st_ref[...] = st * jnp.exp(b_last) + _dot_tn(v, kd)


def _hgrn(p, lb_param, gain, layer, b, s):
    nc = s // HGRN_C
    col0 = 3 * A_W // HEAD_W
    blk = lambda off: pl.BlockSpec((HGRN_C, HEAD_W),
                                   lambda bi, h, c: (bi * nc + c, col0 + off * B_HEADS + h))
    return pl.pallas_call(
        functools.partial(_hgrn_kernel, layer=layer),
        grid=(b, B_HEADS, nc),
        in_specs=[blk(0), blk(1), blk(2), blk(3),
                  pl.BlockSpec((DEPTH, HEAD_W), lambda bi, h, c: (0, h)),
                  pl.BlockSpec((None, 1, HEAD_W), lambda bi, h, c: (layer, 0, 0))],
        out_specs=pl.BlockSpec((HGRN_C, HEAD_W), lambda bi, h, c: (bi * nc + c, h)),
        out_shape=jax.ShapeDtypeStruct((b * s, B_W), BF16),
        scratch_shapes=[pltpu.VMEM((HEAD_W, HEAD_W), F32)],
        compiler_params=_cparams(("parallel", "parallel", "arbitrary")),
        name="hgrn2",
    )(p, p, p, p, lb_param, gain)


def _lane_pick(x, idx):
    lane = lax.broadcasted_iota(jnp.int32, x.shape, 1)
    return jnp.sum(jnp.where(lane == idx, x, 0.0), axis=-1, keepdims=True)


def _gdn_kernel(q_ref, k_ref, v_ref, z_ref, pt_ref, cq_ref, ck_ref, cv_ref, alog_ref, dtb_ref,
                gain_ref, o_ref, xe_ref, s_ref):
    hd = pl.program_id(1)
    c = pl.program_id(2)
    nb, cs, cb = GDN_NB, GDN_C, GDN_CB
    halo = SUBLANES

    @pl.when(c == 0)
    def _():
        s_ref[...] = jnp.zeros(s_ref.shape, F32)
        xe_ref[:, 0:halo, :] = jnp.zeros((3, halo, HEAD_W), F32)

    def conv(idx, x_ref, w_ref):
        xe_ref[idx, halo:halo + cb, :] = x_ref[...]
        w = w_ref[...]
        y = jnp.zeros((cb, HEAD_W), F32)
        for j in range(CONV_K):
            off = halo - (CONV_K - 1) + j
            y = y + w[j:j + 1, :] * xe_ref[idx, off:off + cb, :]
        xe_ref[idx, 0:halo, :] = xe_ref[idx, cb:cb + halo, :]
        return _silu(y)

    def l2n(x):
        return x * lax.rsqrt(jnp.sum(x * x, axis=-1, keepdims=True) + EPS)

    q = l2n(conv(0, q_ref, cq_ref)) * (HEAD_W ** -0.5)
    k = l2n(conv(1, k_ref, ck_ref))
    v = conv(2, v_ref, cv_ref)

    pt = pt_ref[...]
    beta = jax.nn.sigmoid(_lane_pick(pt, hd))
    a_in = _lane_pick(pt, C_HEADS + hd)
    a_log = _lane_pick(alog_ref[...], hd)
    dt_b = _lane_pick(dtb_ref[...], hd)
    xg = a_in + dt_b
    softplus = jnp.maximum(xg, 0.0) + jnp.log(1.0 + jnp.exp(-jnp.abs(xg)))
    g = -jnp.exp(a_log) * softplus

    row = lax.broadcasted_iota(jnp.int32, (cb, cb), 0)
    col = lax.broadcasted_iota(jnp.int32, (cb, cb), 1)
    blk_tril = jnp.logical_and(row >= col, (row & -cs) == (col & -cs)).astype(F32)
    bc = _dot_hi(blk_tril, jnp.broadcast_to(g, (cb, HEAD_W)))

    kb = k * beta
    vb = v * beta
    ebc = jnp.exp(bc)

    ri = lax.broadcasted_iota(jnp.int32, (cs, cs), 0)
    ci = lax.broadcasted_iota(jnp.int32, (cs, cs), 1)
    eye = (ri == ci).astype(F32)
    incl = ri >= ci
    strict = ri > ci
    ones8 = jnp.ones((SUBLANES, cs), F32)

    s_state = s_ref[...]
    outs = []
    for n in range(nb):
        sl = slice(n * cs, (n + 1) * cs)
        bcn = bc[sl]
        bc_row = _dot_hi(ones8, eye * bcn[:, :cs])[0:1, :]
        diff = bcn[:, :cs] - bc_row
        decay = jnp.where(incl, jnp.exp(jnp.where(incl, diff, 0.0)), 0.0)
        kn, kbn = k[sl], kb[sl]
        lmat = jnp.where(strict, _dot_nt(kbn, kn) * decay, 0.0)
        x = eye - lmat
        pw = _dot_hi(lmat, lmat)
        size = 2
        while True:
            x = x + _dot_hi(x, pw)
            size *= 2
            if size >= cs:
                break
            pw = _dot_hi(pw, pw)
        u = _dot_hi(x, vb[sl])
        w = _dot_hi(x, kbn * ebc[sl])
        attn = _dot_nt(q[sl], kn) * decay
        b_last = bcn[cs - 1:cs, :]
        v_new = u - _dot(w, s_state)
        outs.append(_dot(q[sl] * ebc[sl], s_state) + _dot(attn, v_new))
        s_state = s_state * jnp.exp(b_last) + _dot_tn(kn * jnp.exp(b_last - bcn), v_new)
    s_ref[...] = s_state

    o = jnp.concatenate(outs, axis=0)
    o_ref[...] = (_rms(o, gain_ref[...]) * _silu(z_ref[...])).astype(BF16)


def _gdn(p, p_tail, conv_w, a_log, dt_bias, gain, layer, b, s):
    cb = min(GDN_CB, s)
    assert cb == GDN_CB
    nc = s // cb
    col0 = (3 * A_W + 4 * B_W) // HEAD_W
    blk = lambda off: pl.BlockSpec((cb, HEAD_W),
                                   lambda bi, h, c: (bi * nc + c, col0 + off * C_HEADS + h))
    cw = lambda off: pl.BlockSpec((None, CONV_K, HEAD_W),
                                  lambda bi, h, c: (layer, 0, off * C_HEADS + h))
    prow = pl.BlockSpec((None, 1, LANES), lambda bi, h, c: (layer, 0, 0))
    return pl.pallas_call(
        _gdn_kernel,
        grid=(b, C_HEADS, nc),
        in_specs=[blk(0), blk(1), blk(2), blk(3),
                  pl.BlockSpec((cb, LANES), lambda bi, h, c: (bi * nc + c, 0)),
                  cw(0), cw(1), cw(2), prow, prow,
                  pl.BlockSpec((None, 1, HEAD_W), lambda bi, h, c: (layer, 0, 0))],
        out_specs=pl.BlockSpec((cb, HEAD_W), lambda bi, h, c: (bi * nc + c, h)),
        out_shape=jax.ShapeDtypeStruct((b * s, C_W), BF16),
        scratch_shapes=[pltpu.VMEM((3, SUBLANES + cb, HEAD_W), F32),
                        pltpu.VMEM((HEAD_W, HEAD_W), F32)],
        compiler_params=_cparams(("parallel", "parallel", "arbitrary")),
        name="gdn",
    )(p, p, p, p, p_tail, conv_w, conv_w, conv_w, a_log, dt_bias, gain)


def kernel(x, ffn1_norm, ffn1_w_gate, ffn1_w_up, ffn1_w_down, mix_norm, w_in, w_out, lambda_q1, lambda_k1, lambda_q2, lambda_k2, diff_gain, hgrn_lb_param, hgrn_gain, gdn_conv_w, gdn_a_log, gdn_dt_bias, gdn_gain, ffn2_norm, ffn2_w_gate, ffn2_w_up, ffn2_w_down, final_norm):
    b, s, d = x.shape
    depth = w_in.shape[0]
    xt = x.reshape(b * s, d)

    w_tail = jnp.pad(w_in[:, :, P_MAIN:], ((0, 0), (0, 0), (0, LANES - P_TAIL)))
    lam_p = jnp.stack([lambda_q1, lambda_k1, lambda_q2, lambda_k2], axis=1)
    rows = lambda t: t[:, None, :]
    pad_heads = lambda t: rows(jnp.pad(t, ((0, 0), (0, LANES - t.shape[1]))))
    a_log_p = pad_heads(gdn_a_log)
    dt_bias_p = pad_heads(gdn_dt_bias)
    ffn1_norm, mix_norm, ffn2_norm = rows(ffn1_norm), rows(mix_norm), rows(ffn2_norm)
    diff_g, hgrn_g, gdn_g = rows(diff_gain), rows(hgrn_gain), rows(gdn_gain)
    final_w = final_norm.reshape(1, d)
    cos_t, sin_t = _rope_tables(s)

    for l in range(depth):
        xt = _ffn(xt, ffn1_norm, ffn1_w_gate, ffn1_w_up, ffn1_w_down, final_w, l, False)
        p, p_tail = _inproj(xt, mix_norm, w_in, w_tail, l)
        qr, k0, k1, vb = _attn_prep(p, cos_t, sin_t, s)
        oa = _attn(qr, k0, k1, vb, lam_p, diff_g, l, b, s)
        ob = _hgrn(p, hgrn_lb_param, hgrn_g, l, b, s)
        oc = _gdn(p, p_tail, gdn_conv_w, a_log_p, dt_bias_p, gdn_g, l, b, s)
        xt = _outproj(xt, oa, ob, oc, w_out, l)
        xt = _ffn(xt, ffn2_norm, ffn2_w_gate, ffn2_w_up, ffn2_w_down, final_w, l, l == depth - 1)
    return xt.reshape(b, s, d)
```

```python
import functools
import math

import jax
import jax.numpy as jnp
from jax import lax
from jax.experimental import pallas as pl
from jax.experimental.pallas import tpu as pltpu

F32 = jnp.float32
BF16 = jnp.bfloat16
HIGHEST = lax.Precision.HIGHEST

D_MODEL = 2048
DEPTH = 4
A_HEADS = 4
A_QK_DIM = 64
A_V_DIM = 128
ROPE_THETA = 10000.0
B_HEADS = 6
C_HEADS = 6
HEAD_W = 128
CONV_K = 4
D_FF = 5632
EPS = 1e-6
A_W = A_HEADS * HEAD_W
B_W = B_HEADS * HEAD_W
C_W = C_HEADS * HEAD_W
P_MAIN = 3 * A_W + 4 * B_W + 4 * C_W
P_TAIL = 2 * C_HEADS

LANES = 128
SUBLANES = 8
VMEM_LIMIT_BYTES = 56 * 1024 * 1024

FFN_TM = 1024
FFN_TF = 256
INPROJ_TM = 1024
INPROJ_TN = 768
OUTPROJ_TM = 256
PREP_TS = 512
ATT_TQ = 1024
ATT_TK = 512
HGRN_C = 128
HGRN_SUB = 16
HGRN_HPS = 6
GDN_C = 64
GDN_NB = 4
GDN_CB = GDN_C * GDN_NB
GDN_HPS = 6


def _cparams(sem):
    return pltpu.CompilerParams(dimension_semantics=sem, vmem_limit_bytes=VMEM_LIMIT_BYTES)


def _rms(x, w):
    return x * lax.rsqrt(jnp.mean(x * x, axis=-1, keepdims=True) + EPS) * w


def _silu(x):
    return x * jax.nn.sigmoid(x)


def _dot(a, b):
    return jnp.dot(a.astype(BF16), b.astype(BF16), preferred_element_type=F32)


def _dot_nt(a, b):
    return lax.dot_general(a.astype(BF16), b.astype(BF16), (((1,), (1,)), ((), ())),
                           preferred_element_type=F32)


def _dot_tn(a, b):
    return lax.dot_general(a.astype(BF16), b.astype(BF16), (((0,), (0,)), ((), ())),
                           preferred_element_type=F32)


def _split(a):
    hi = a.astype(BF16)
    return hi, (a - hi.astype(F32)).astype(BF16)


def _mm(a, b):
    return jnp.dot(a, b, preferred_element_type=F32)


def _dot3(a, b):
    ah, al = _split(a)
    bh, bl = _split(b)
    return _mm(ah, bh) + (_mm(ah, bl) + _mm(al, bh))


def _dot_left01(a01, x):
    a = a01.astype(BF16)
    x0 = x.astype(BF16)
    r = x - x0.astype(F32)
    x1 = r.astype(BF16)
    x2 = (r - x1.astype(F32)).astype(BF16)
    return _mm(a, x0) + (_mm(a, x1) + _mm(a, x2))


def _ffn_kernel(x_ref, nw_ref, wg_ref, wu_ref, wd_ref, fw_ref, o_ref, h_ref, *, final_norm):
    j = pl.program_id(1)

    @pl.when(j == 0)
    def _():
        x = x_ref[...]
        h_ref[...] = _rms(x, nw_ref[...]).astype(BF16)
        o_ref[...] = x

    h = h_ref[...]
    g = jnp.dot(h, wg_ref[...].astype(BF16), preferred_element_type=F32)
    u = jnp.dot(h, wu_ref[...].astype(BF16), preferred_element_type=F32)
    a = (0.5 * _silu(g) * u).astype(BF16)
    o_ref[...] += jnp.dot(a, wd_ref[...].astype(BF16), preferred_element_type=F32)

    if final_norm:
        @pl.when(j == pl.num_programs(1) - 1)
        def _():
            o_ref[...] = _rms(o_ref[...], fw_ref[...])


def _ffn(x, norm_w, w_gate, w_up, w_down, final_w, layer, final_norm):
    t = x.shape[0]
    tm = min(FFN_TM, t)
    grid = (t // tm, D_FF // FFN_TF)
    return pl.pallas_call(
        functools.partial(_ffn_kernel, final_norm=final_norm),
        grid=grid,
        in_specs=[
            pl.BlockSpec((tm, D_MODEL), lambda i, j: (i, 0)),
            pl.BlockSpec((None, 1, D_MODEL), lambda i, j: (layer, 0, 0)),
            pl.BlockSpec((None, D_MODEL, FFN_TF), lambda i, j: (layer, 0, j)),
            pl.BlockSpec((None, D_MODEL, FFN_TF), lambda i, j: (layer, 0, j)),
            pl.BlockSpec((None, FFN_TF, D_MODEL), lambda i, j: (layer, j, 0)),
            pl.BlockSpec((1, D_MODEL), lambda i, j: (0, 0)),
        ],
        out_specs=pl.BlockSpec((tm, D_MODEL), lambda i, j: (i, 0)),
        out_shape=jax.ShapeDtypeStruct((t, D_MODEL), F32),
        scratch_shapes=[pltpu.VMEM((tm, D_MODEL), BF16)],
        compiler_params=_cparams(("parallel", "arbitrary")),
        name="ffn",
    )(x, norm_w, w_gate, w_up, w_down, final_w)


def _inproj_kernel(x_ref, nw_ref, w_ref, wt_ref, p_ref, pt_ref, h_ref):
    j = pl.program_id(1)

    @pl.when(j == 0)
    def _():
        h = _rms(x_ref[...], nw_ref[...]).astype(BF16)
        h_ref[...] = h
        pt_ref[...] = jnp.dot(h, wt_ref[...].astype(BF16), preferred_element_type=F32)

    p_ref[...] = jnp.dot(h_ref[...], w_ref[...].astype(BF16), preferred_element_type=F32)


def _inproj(x, norm_w, w_in, w_tail, layer):
    t = x.shape[0]
    tm = min(INPROJ_TM, t)
    grid = (t // tm, P_MAIN // INPROJ_TN)
    return pl.pallas_call(
        _inproj_kernel,
        grid=grid,
        in_specs=[
            pl.BlockSpec((tm, D_MODEL), lambda i, j: (i, 0)),
            pl.BlockSpec((None, 1, D_MODEL), lambda i, j: (layer, 0, 0)),
            pl.BlockSpec((None, D_MODEL, INPROJ_TN), lambda i, j: (layer, 0, j)),
            pl.BlockSpec((None, D_MODEL, LANES), lambda i, j: (layer, 0, 0)),
        ],
        out_specs=[
            pl.BlockSpec((tm, INPROJ_TN), lambda i, j: (i, j)),
            pl.BlockSpec((tm, LANES), lambda i, j: (i, 0)),
        ],
        out_shape=[jax.ShapeDtypeStruct((t, P_MAIN), F32),
                   jax.ShapeDtypeStruct((t, LANES), F32)],
        scratch_shapes=[pltpu.VMEM((tm, D_MODEL), BF16)],
        compiler_params=_cparams(("parallel", "arbitrary")),
        name="inproj",
    )(x, norm_w, w_in, w_tail)


def _outproj_kernel(x_ref, oa_ref, ob_ref, oc_ref, w_ref, o_ref, wb_ref):
    @pl.when(pl.program_id(0) == 0)
    def _():
        wb_ref[...] = w_ref[...].astype(BF16)

    acc = x_ref[...] + jnp.dot(oa_ref[...], wb_ref[0:A_W, :], preferred_element_type=F32)
    acc = acc + jnp.dot(ob_ref[...], wb_ref[A_W:A_W + B_W, :], preferred_element_type=F32)
    o_ref[...] = acc + jnp.dot(oc_ref[...], wb_ref[A_W + B_W:, :], preferred_element_type=F32)


def _outproj(x, oa, ob, oc, w_out, layer):
    t = x.shape[0]
    tm = min(OUTPROJ_TM, t)
    row = lambda w: pl.BlockSpec((tm, w), lambda i: (i, 0))
    return pl.pallas_call(
        _outproj_kernel,
        grid=(t // tm,),
        in_specs=[row(D_MODEL), row(A_W), row(B_W), row(C_W),
                  pl.BlockSpec((None, D_MODEL, D_MODEL), lambda i: (layer, 0, 0))],
        out_specs=row(D_MODEL),
        out_shape=jax.ShapeDtypeStruct((t, D_MODEL), F32),
        scratch_shapes=[pltpu.VMEM((D_MODEL, D_MODEL), BF16)],
        compiler_params=_cparams(("arbitrary",)),
        name="outproj",
    )(x, oa, ob, oc, w_out)


def _rope_tables(s):
    half = A_QK_DIM // 2
    inv_freq = 1.0 / (ROPE_THETA ** (jnp.arange(half, dtype=F32) / half))
    ang = jnp.arange(s).astype(F32)[:, None] * inv_freq[None, :]
    cos, sin = jnp.cos(ang), jnp.sin(ang)
    cos_t = jnp.concatenate([cos, cos, cos, cos], axis=-1)
    sin_t = jnp.concatenate([-sin, sin, -sin, sin], axis=-1)
    return cos_t, sin_t


def _attn_prep_kernel(q_ref, k_ref, v_ref, cos_ref, sin_ref, qo_ref, k0_ref, k1_ref, vo_ref):
    cos = cos_ref[...]
    sin = sin_ref[...]
    lane = lax.broadcasted_iota(jnp.int32, cos.shape, 1)
    first_half = (lane & (A_QK_DIM - 1)) < (A_QK_DIM // 2)
    comp0 = lane < A_QK_DIM
    half = A_QK_DIM // 2

    def rope(t):
        swapped = jnp.where(first_half, pltpu.roll(t, LANES - half, 1), pltpu.roll(t, half, 1))
        return t * cos + swapped * sin

    for h in range(A_HEADS):
        sl = slice(h * HEAD_W, (h + 1) * HEAD_W)
        qo_ref[:, sl] = (rope(q_ref[:, sl]) * (A_QK_DIM ** -0.5)).astype(BF16)
        kr = rope(k_ref[:, sl])
        k0_ref[:, sl] = jnp.where(comp0, kr, 0.0).astype(BF16)
        k1_ref[:, sl] = jnp.where(comp0, 0.0, kr).astype(BF16)
    vo_ref[...] = v_ref[...].astype(BF16)


def _attn_prep(p, cos_t, sin_t, s):
    t = p.shape[0]
    ts = min(PREP_TS, s)
    ns = s // ts
    blk = lambda c: pl.BlockSpec((ts, A_W), lambda i: (i, c))
    tab = pl.BlockSpec((ts, LANES), lambda i: (i % ns, 0))
    out = jax.ShapeDtypeStruct((t, A_W), BF16)
    return pl.pallas_call(
        _attn_prep_kernel,
        grid=(t // ts,),
        in_specs=[blk(0), blk(1), blk(2), tab, tab],
        out_specs=[pl.BlockSpec((ts, A_W), lambda i: (i, 0))] * 4,
        out_shape=[out] * 4,
        compiler_params=_cparams(("parallel",)),
        name="attn_prep",
    )(p, p, p, cos_t, sin_t)


def _attn_kernel(q_ref, k0_ref, k1_ref, v_ref, lam_ref, gain_ref, o_ref,
                 m_ref, l_ref, acc_ref, *, lambda_init, tq, tk):
    qi = pl.program_id(2)
    q = q_ref[...]
    m_ref[...] = jnp.full(m_ref.shape, -jnp.inf, F32)
    l_ref[...] = jnp.zeros(l_ref.shape, F32)
    acc_ref[...] = jnp.zeros(acc_ref.shape, F32)

    def tile(kt, masked):
        ks = pl.ds(pl.multiple_of(kt * tk, tk), tk)
        v = v_ref[ks, :]
        maps = (0, 1)
        s = [_dot_nt(k_ref[ks, :], q) for k_ref in (k0_ref, k1_ref)]
        if masked:
            kidx = kt * tk + lax.broadcasted_iota(jnp.int32, (tk, tq), 0)
            qidx = qi * tq + lax.broadcasted_iota(jnp.int32, (tk, tq), 1)
            s = [jnp.where(kidx <= qidx, s[c], -jnp.inf) for c in maps]
        m_old = [m_ref[c] for c in maps]
        m_new = [jnp.maximum(m_old[c], jnp.max(s[c], axis=0, keepdims=True)) for c in maps]
        alpha = [jnp.exp(m_old[c] - m_new[c]) for c in maps]
        p = [jnp.exp(s[c] - m_new[c]) for c in maps]
        for c in maps:
            l_ref[c] = alpha[c] * l_ref[c] + jnp.sum(p[c], axis=0, keepdims=True)
            m_ref[c] = m_new[c]
        pv = [_dot_tn(v, p[c]) for c in maps]
        for c in maps:
            acc_ref[c] = alpha[c] * acc_ref[c] + pv[c]

    n_full = qi * (tq // tk)

    def body(kt, carry):
        tile(kt, False)
        return carry

    lax.fori_loop(0, n_full, body, 0)
    for d in range(tq // tk):
        tile(n_full + d, True)

    lp = lam_ref[...]
    lam = (jnp.exp(jnp.sum(lp[0:1] * lp[1:2], axis=-1, keepdims=True))
           - jnp.exp(jnp.sum(lp[2:3] * lp[3:4], axis=-1, keepdims=True)) + lambda_init)
    o_t = acc_ref[0] / l_ref[0] - lam * (acc_ref[1] / l_ref[1])
    o_ref[...] = (_rms(o_t.T, gain_ref[...]) * (1.0 - lambda_init)).astype(BF16)


def _attn(qr, k0, k1, vb, lam_p, gain, layer, b, s):
    tq = min(ATT_TQ, s)
    tk = min(ATT_TK, tq)
    nq = s // tq
    lambda_init = 0.8 - 0.6 * math.exp(-0.3 * layer)
    qmap = lambda bi, h, qi: (bi * nq + qi, h)
    kmap = lambda bi, h, qi: (bi, h)
    return pl.pallas_call(
        functools.partial(_attn_kernel, lambda_init=lambda_init, tq=tq, tk=tk),
        grid=(b, A_HEADS, nq),
        in_specs=[
            pl.BlockSpec((tq, HEAD_W), qmap),
            pl.BlockSpec((s, HEAD_W), kmap),
            pl.BlockSpec((s, HEAD_W), kmap),
            pl.BlockSpec((s, HEAD_W), kmap),
            pl.BlockSpec((None, 4, A_QK_DIM), lambda bi, h, qi: (layer, 0, 0)),
            pl.BlockSpec((None, 1, A_V_DIM), lambda bi, h, qi: (layer, 0, 0)),
        ],
        out_specs=pl.BlockSpec((tq, HEAD_W), qmap),
        out_shape=jax.ShapeDtypeStruct((b * s, A_W), BF16),
        scratch_shapes=[pltpu.VMEM((2, 1, tq), F32), pltpu.VMEM((2, 1, tq), F32),
                        pltpu.VMEM((2, A_V_DIM, tq), F32)],
        compiler_params=_cparams(("parallel", "parallel", "arbitrary")),
        name="diff_attn",
    )(qr, k0, k1, vb, lam_p, gain)


def _group_row(x, group, row):
    n = x.shape[0]
    xg = x.reshape(n // group, group, x.shape[1])
    return jnp.broadcast_to(xg[:, row:row + 1, :], xg.shape).reshape(x.shape)


def _hgrn_kernel(q_ref, f_ref, i_ref, g_ref, lbp_ref, gain_ref, o_ref, st_ref, *, layer):
    c = pl.program_id(2)
    n = HGRN_C

    @pl.when(c == 0)
    def _():
        st_ref[...] = jnp.zeros(st_ref.shape, F32)

    lp = lbp_ref[...]
    e = jnp.exp(lp - jnp.max(lp, axis=0, keepdims=True))
    sm = e / jnp.sum(e, axis=0, keepdims=True)
    lb_all = jnp.zeros((1, HGRN_HPS * HEAD_W), F32)
    for r in range(1, layer + 1):
        lb_all = lb_all + sm[r:r + 1]

    row = lax.broadcasted_iota(jnp.int32, (n, n), 0)
    col = lax.broadcasted_iota(jnp.int32, (n, n), 1)
    tok = lax.broadcasted_iota(jnp.int32, (n, HEAD_W), 0)
    tril = (row >= col).astype(F32)
    gain = gain_ref[...]
    lanes = [slice(hh * HEAD_W, (hh + 1) * HEAD_W) for hh in range(HGRN_HPS)]

    hs = []
    for hh, ln in enumerate(lanes):
        lb = lb_all[:, ln]
        fg = lb + (1.0 - lb) * jax.nn.sigmoid(f_ref[:, ln])
        hs.append(dict(fg=fg, kk=1.0 - fg, qs=_silu(q_ref[:, ln]), v=i_ref[:, ln],
                       st=st_ref[hh]))
    for h in hs:
        h["b"] = _dot_left01(tril, jnp.log(h["fg"]))
    for h in hs:
        h["o"] = _dot_nt(h["qs"] * jnp.exp(h["b"]), h["st"])
        h["a"] = jnp.zeros((n, n), F32)

    half = n // 2
    while half >= HGRN_SUB:
        upper = (tok & half) != 0
        same = (row & -(2 * half)) == (col & -(2 * half))
        for h in hs:
            b = h["b"]
            bref = _group_row(b, 2 * half, half - 1)
            qh = jnp.where(upper, h["qs"] * jnp.exp(jnp.where(upper, b - bref, 0.0)), 0.0)
            kh = jnp.where(upper, 0.0, h["kk"] * jnp.exp(jnp.where(upper, 0.0, bref - b)))
            h["a"] = h["a"] + jnp.where(same, _dot_nt(qh, kh), 0.0)
        half //= 2
    for h in hs:
        h["o"] = h["o"] + _dot(h["a"], h["v"])

    tmod = tok & (HGRN_SUB - 1)
    for j in range(HGRN_SUB):
        valid = tmod >= j
        for h in hs:
            kj = _group_row(h["kk"], HGRN_SUB, j)
            bj = _group_row(h["b"], HGRN_SUB, j)
            vj = _group_row(h["v"], HGRN_SUB, j)
            z = jnp.where(valid, h["qs"] * kj * jnp.exp(jnp.where(valid, h["b"] - bj, 0.0)), 0.0)
            h["o"] = h["o"] + jnp.sum(z, axis=-1, keepdims=True) * vj

    for hh, (h, ln) in enumerate(zip(hs, lanes)):
        o_ref[:, ln] = (_rms(h["o"], gain) * _silu(g_ref[:, ln])).astype(BF16)
        b_last = h["b"][n - 1:n, :]
        kd = h["kk"] * jnp.exp(b_last - h["b"])
        st_ref[hh] = h["st"] * jnp.exp(b_last) + _dot_tn(h["v"], kd)


def _hgrn(p, lb_param, gain, layer, b, s):
    nc = s // HGRN_C
    hw = HGRN_HPS * HEAD_W
    ngrp = B_HEADS // HGRN_HPS
    assert B_HEADS % HGRN_HPS == 0 and (3 * A_W) % hw == 0
    col0 = 3 * A_W // hw
    blk = lambda off: pl.BlockSpec((HGRN_C, hw),
                                   lambda bi, h, c: (bi * nc + c, col0 + off * ngrp + h))
    return pl.pallas_call(
        functools.partial(_hgrn_kernel, layer=layer),
        grid=(b, ngrp, nc),
        in_specs=[blk(0), blk(1), blk(2), blk(3),
                  pl.BlockSpec((DEPTH, hw), lambda bi, h, c: (0, h)),
                  pl.BlockSpec((None, 1, HEAD_W), lambda bi, h, c: (layer, 0, 0))],
        out_specs=pl.BlockSpec((HGRN_C, hw), lambda bi, h, c: (bi * nc + c, h)),
        out_shape=jax.ShapeDtypeStruct((b * s, B_W), BF16),
        scratch_shapes=[pltpu.VMEM((HGRN_HPS, HEAD_W, HEAD_W), F32)],
        compiler_params=_cparams(("parallel", "parallel", "arbitrary")),
        name="hgrn2",
    )(p, p, p, p, lb_param, gain)


def _lane_pick(x, idx):
    lane = lax.broadcasted_iota(jnp.int32, x.shape, 1)
    return jnp.sum(jnp.where(lane == idx, x, 0.0), axis=-1, keepdims=True)


def _gdn_kernel(q_ref, k_ref, v_ref, z_ref, pt_ref, cq_ref, ck_ref, cv_ref, alog_ref, dtb_ref,
                gain_ref, o_ref, xe_ref, s_ref):
    hg = pl.program_id(1)
    c = pl.program_id(2)
    nb, cs, cb = GDN_NB, GDN_C, GDN_CB
    halo = SUBLANES

    @pl.when(c == 0)
    def _():
        s_ref[...] = jnp.zeros(s_ref.shape, F32)
        xe_ref[:, 0:halo, :] = jnp.zeros((3 * GDN_HPS, halo, HEAD_W), F32)

    row = lax.broadcasted_iota(jnp.int32, (cb, cb), 0)
    col = lax.broadcasted_iota(jnp.int32, (cb, cb), 1)
    same = (row & -cs) == (col & -cs)
    incl = jnp.logical_and(same, row >= col)
    strict = jnp.logical_and(same, row > col)
    eye = (row == col).astype(F32)
    tril01 = incl.astype(F32)
    ones8 = jnp.ones((SUBLANES, cb), F32)
    pt = pt_ref[...]
    alog_row = alog_ref[...]
    dtb_row = dtb_ref[...]
    gain = gain_ref[...]

    def l2n(x):
        return x * lax.rsqrt(jnp.sum(x * x, axis=-1, keepdims=True) + EPS)

    heads = range(GDN_HPS)
    lanes = [slice(hh * HEAD_W, (hh + 1) * HEAD_W) for hh in heads]

    def conv(hh, idx, x_ref, w_ref):
        slot = 3 * hh + idx
        xe_ref[slot, halo:halo + cb, :] = x_ref[:, lanes[hh]]
        w = w_ref[:, lanes[hh]]
        y = jnp.zeros((cb, HEAD_W), F32)
        for j in range(CONV_K):
            off = halo - (CONV_K - 1) + j
            y = y + w[j:j + 1, :] * xe_ref[slot, off:off + cb, :]
        xe_ref[slot, 0:halo, :] = xe_ref[slot, cb:cb + halo, :]
        return _silu(y)

    hs = []
    for hh in heads:
        hd = hg * GDN_HPS + hh
        q = l2n(conv(hh, 0, q_ref, cq_ref)) * (HEAD_W ** -0.5)
        k = l2n(conv(hh, 1, k_ref, ck_ref))
        v = conv(hh, 2, v_ref, cv_ref)
        beta = jax.nn.sigmoid(_lane_pick(pt, hd))
        xg = _lane_pick(pt, C_HEADS + hd) + _lane_pick(dtb_row, hd)
        softplus = jnp.maximum(xg, 0.0) + jnp.log(1.0 + jnp.exp(-jnp.abs(xg)))
        g = -jnp.exp(_lane_pick(alog_row, hd)) * softplus
        hs.append(dict(q=q, k=k, kb=k * beta, vb=v * beta, g=g))

    for h in hs:
        h["bc"] = _dot_left01(tril01, jnp.broadcast_to(h["g"], (cb, HEAD_W)))
    for h in hs:
        bc2 = jnp.concatenate([h["bc"], h["bc"]], axis=1)
        bc_row = _dot_left01(ones8, eye * bc2)[0:1, :]
        h["decay"] = jnp.where(incl, jnp.exp(jnp.where(incl, bc2 - bc_row, 0.0)), 0.0)
        h["ebc"] = jnp.exp(h["bc"])
    for h in hs:
        h["lmat"] = jnp.where(strict, _dot_nt(h["kb"], h["k"]) * h["decay"], 0.0)
        h["attn"] = _dot_nt(h["q"], h["k"]) * h["decay"]

    for h in hs:
        h["x"] = eye - h["lmat"]
        h["pw"] = _dot3(h["lmat"], h["lmat"])
    size = 2
    while True:
        size *= 2
        if size >= cs:
            for h in hs:
                h["x"] = h["x"] + _dot3(h["x"], h["pw"])
            break
        for h in hs:
            both = _dot3(jnp.concatenate([h["x"], h["pw"]], axis=0), h["pw"])
            h["x"] = h["x"] + both[:cb]
            h["pw"] = both[cb:]
    for h in hs:
        uw = _dot3(h["x"], jnp.concatenate([h["vb"], h["kb"] * h["ebc"]], axis=1))
        h["u"] = uw[:, :HEAD_W]
        h["w"] = uw[:, HEAD_W:]
        h["qe"] = h["q"] * h["ebc"]
        h["v_news"], h["o_inter"] = [], []

    for hh, h in enumerate(hs):
        h["s"] = s_ref[hh]
    for n in range(nb):
        sl = slice(n * cs, (n + 1) * cs)
        for h in hs:
            bc = h["bc"]
            b_last = bc[(n + 1) * cs - 1:(n + 1) * cs, :]
            wq = _dot(jnp.concatenate([h["w"][sl], h["qe"][sl]], axis=0), h["s"])
            v_new = h["u"][sl] - wq[:cs]
            h["o_inter"].append(wq[cs:])
            h["v_news"].append(v_new)
            h["s"] = (h["s"] * jnp.exp(b_last)
                      + _dot_tn(h["k"][sl] * jnp.exp(b_last - bc[sl]), v_new))
    for hh, h in enumerate(hs):
        s_ref[hh] = h["s"]
        o = (jnp.concatenate(h["o_inter"], axis=0)
             + _dot(h["attn"], jnp.concatenate(h["v_news"], axis=0)))
        o_ref[:, lanes[hh]] = (_rms(o, gain) * _silu(z_ref[:, lanes[hh]])).astype(BF16)


def _gdn(p, p_tail, conv_w, a_log, dt_bias, gain, layer, b, s):
    cb = GDN_CB
    assert cb == 2 * HEAD_W and s % cb == 0 and C_HEADS % GDN_HPS == 0
    nc = s // cb
    hw = GDN_HPS * HEAD_W
    col0 = (3 * A_W + 4 * B_W) // hw
    blk = lambda off: pl.BlockSpec(
        (cb, hw), lambda bi, h, c: (bi * nc + c, col0 + off * (C_HEADS // GDN_HPS) + h))
    cw = lambda off: pl.BlockSpec(
        (None, CONV_K, hw), lambda bi, h, c: (layer, 0, off * (C_HEADS // GDN_HPS) + h))
    prow = pl.BlockSpec((None, 1, LANES), lambda bi, h, c: (layer, 0, 0))
    return pl.pallas_call(
        _gdn_kernel,
        grid=(b, C_HEADS // GDN_HPS, nc),
        in_specs=[blk(0), blk(1), blk(2), blk(3),
                  pl.BlockSpec((cb, LANES), lambda bi, h, c: (bi * nc + c, 0)),
                  cw(0), cw(1), cw(2), prow, prow,
                  pl.BlockSpec((None, 1, HEAD_W), lambda bi, h, c: (layer, 0, 0))],
        out_specs=pl.BlockSpec((cb, hw), lambda bi, h, c: (bi * nc + c, h)),
        out_shape=jax.ShapeDtypeStruct((b * s, C_W), BF16),
        scratch_shapes=[pltpu.VMEM((3 * GDN_HPS, SUBLANES + cb, HEAD_W), F32),
                        pltpu.VMEM((GDN_HPS, HEAD_W, HEAD_W), F32)],
        compiler_params=_cparams(("parallel", "parallel", "arbitrary")),
        name="gdn",
    )(p, p, p, p, p_tail, conv_w, conv_w, conv_w, a_log, dt_bias, gain)


def kernel(x, ffn1_norm, ffn1_w_gate, ffn1_w_up, ffn1_w_down, mix_norm, w_in, w_out, lambda_q1, lambda_k1, lambda_q2, lambda_k2, diff_gain, hgrn_lb_param, hgrn_gain, gdn_conv_w, gdn_a_log, gdn_dt_bias, gdn_gain, ffn2_norm, ffn2_w_gate, ffn2_w_up, ffn2_w_down, final_norm):
    b, s, d = x.shape
    depth = w_in.shape[0]
    xt = x.reshape(b * s, d)

    w_tail = jnp.pad(w_in[:, :, P_MAIN:], ((0, 0), (0, 0), (0, LANES - P_TAIL)))
    lam_p = jnp.stack([lambda_q1, lambda_k1, lambda_q2, lambda_k2], axis=1)
    rows = lambda t: t[:, None, :]
    pad_heads = lambda t: rows(jnp.pad(t, ((0, 0), (0, LANES - t.shape[1]))))
    a_log_p = pad_heads(gdn_a_log)
    dt_bias_p = pad_heads(gdn_dt_bias)
    ffn1_norm, mix_norm, ffn2_norm = rows(ffn1_norm), rows(mix_norm), rows(ffn2_norm)
    diff_g, hgrn_g, gdn_g = rows(diff_gain), rows(hgrn_gain), rows(gdn_gain)
    final_w = final_norm.reshape(1, d)
    cos_t, sin_t = _rope_tables(s)

    for l in range(depth):
        xt = _ffn(xt, ffn1_norm, ffn1_w_gate, ffn1_w_up, ffn1_w_down, final_w, l, False)
        p, p_tail = _inproj(xt, mix_norm, w_in, w_tail, l)
        qr, k0, k1, vb = _attn_prep(p, cos_t, sin_t, s)
        oa = _attn(qr, k0, k1, vb, lam_p, diff_g, l, b, s)
        ob = _hgrn(p, hgrn_lb_param, hgrn_g, l, b, s)
        oc = _gdn(p, p_tail, gdn_conv_w, a_log_p, dt_bias_p, gdn_g, l, b, s)
        xt = _outproj(xt, oa, ob, oc, w_out, l)
        xt = _ffn(xt, ffn2_norm, ffn2_w_gate, ffn2_w_up, ffn2_w_down, final_w, l, l == depth - 1)
    return xt.reshape(b, s, d)
```

```python
import functools
import math

import jax
import jax.numpy as jnp
from jax import lax
from jax.experimental import pallas as pl
from jax.experimental.pallas import tpu as pltpu

F32 = jnp.float32
BF16 = jnp.bfloat16
HIGHEST = lax.Precision.HIGHEST

D_MODEL = 2048
DEPTH = 4
A_HEADS = 4
A_QK_DIM = 64
A_V_DIM = 128
ROPE_THETA = 10000.0
B_HEADS = 6
C_HEADS = 6
HEAD_W = 128
CONV_K = 4
D_FF = 5632
EPS = 1e-6
A_W = A_HEADS * HEAD_W
B_W = B_HEADS * HEAD_W
C_W = C_HEADS * HEAD_W
P_MAIN = 3 * A_W + 4 * B_W + 4 * C_W
P_TAIL = 2 * C_HEADS
P_PAD = P_MAIN + 128

LANES = 128
SUBLANES = 8
VMEM_LIMIT_BYTES = 56 * 1024 * 1024

FFN_TM = 1024
FFN_TF = 256
INPROJ_TM = 1024
INPROJ_TN = 768
WPREP_TN = 128
OUTPROJ_TM = 256
PREP_TS = 512
ATT_TQ = 1024
ATT_TK = 512
ATT_HPS = 2
HGRN_C = 128
HGRN_SUB = 4
HGRN_HPS = 6
GDN_C = 64
GDN_NB = 4
GDN_CB = GDN_C * GDN_NB
GDN_HPS = 6


def _cparams(sem):
    return pltpu.CompilerParams(dimension_semantics=sem, vmem_limit_bytes=VMEM_LIMIT_BYTES)


def _rms(x, w):
    return x * lax.rsqrt(jnp.mean(x * x, axis=-1, keepdims=True) + EPS) * w


def _silu(x):
    return x * jax.nn.sigmoid(x)


def _dot(a, b):
    return jnp.dot(a.astype(BF16), b.astype(BF16), preferred_element_type=F32)


def _dot_nt(a, b):
    return lax.dot_general(a.astype(BF16), b.astype(BF16), (((1,), (1,)), ((), ())),
                           preferred_element_type=F32)


def _dot_tn(a, b):
    return lax.dot_general(a.astype(BF16), b.astype(BF16), (((0,), (0,)), ((), ())),
                           preferred_element_type=F32)


def _split(a):
    hi = a.astype(BF16)
    return hi, (a - hi.astype(F32)).astype(BF16)


def _mm(a, b):
    return jnp.dot(a, b, preferred_element_type=F32)


def _dot3s(ah, al, bh, bl):
    return _mm(jnp.concatenate([ah, ah, al], axis=1), jnp.concatenate([bh, bl, bh], axis=0))


def _dot3(a, b):
    return _dot3s(*_split(a), *_split(b))


def _dot_left01(a_bf16, x):
    x0 = x.astype(BF16)
    r = x - x0.astype(F32)
    x1 = r.astype(BF16)
    x2 = (r - x1.astype(F32)).astype(BF16)
    return _mm(a_bf16, x0) + (_mm(a_bf16, x1) + _mm(a_bf16, x2))


def _ffn_kernel(x_ref, nw_ref, wg_ref, wu_ref, wd_ref, fw_ref, o_ref, h_ref, *, final_norm):
    j = pl.program_id(1)

    @pl.when(j == 0)
    def _():
        x = x_ref[...]
        h_ref[...] = _rms(x, nw_ref[...]).astype(BF16)
        o_ref[...] = x

    h = h_ref[...]
    g = jnp.dot(h, wg_ref[...].astype(BF16), preferred_element_type=F32)
    u = jnp.dot(h, wu_ref[...].astype(BF16), preferred_element_type=F32)
    a = (0.5 * _silu(g) * u).astype(BF16)
    o_ref[...] += jnp.dot(a, wd_ref[...].astype(BF16), preferred_element_type=F32)

    if final_norm:
        @pl.when(j == pl.num_programs(1) - 1)
        def _():
            o_ref[...] = _rms(o_ref[...], fw_ref[...])


def _ffn(x, norm_w, w_gate, w_up, w_down, final_w, layer, final_norm):
    t = x.shape[0]
    tm = min(FFN_TM, t)
    grid = (t // tm, D_FF // FFN_TF)
    return pl.pallas_call(
        functools.partial(_ffn_kernel, final_norm=final_norm),
        grid=grid,
        in_specs=[
            pl.BlockSpec((tm, D_MODEL), lambda i, j: (i, 0)),
            pl.BlockSpec((None, 1, D_MODEL), lambda i, j: (layer, 0, 0)),
            pl.BlockSpec((None, D_MODEL, FFN_TF), lambda i, j: (layer, 0, j)),
            pl.BlockSpec((None, D_MODEL, FFN_TF), lambda i, j: (layer, 0, j)),
            pl.BlockSpec((None, FFN_TF, D_MODEL), lambda i, j: (layer, j, 0)),
            pl.BlockSpec((1, D_MODEL), lambda i, j: (0, 0)),
        ],
        out_specs=pl.BlockSpec((tm, D_MODEL), lambda i, j: (i, 0)),
        out_shape=jax.ShapeDtypeStruct((t, D_MODEL), F32),
        scratch_shapes=[pltpu.VMEM((tm, D_MODEL), BF16)],
        compiler_params=_cparams(("parallel", "arbitrary")),
        name="ffn",
    )(x, norm_w, w_gate, w_up, w_down, final_w)


def _wprep_kernel(w_ref, o_ref):
    j = pl.program_id(0)
    rows = j * WPREP_TN + lax.broadcasted_iota(jnp.int32, (WPREP_TN, D_MODEL), 0)
    for l in range(DEPTH):
        o_ref[l] = jnp.where(rows < P_MAIN + P_TAIL, w_ref[:, l, :], 0.0).astype(BF16)


def _wprep(w_in_t):
    return pl.pallas_call(
        _wprep_kernel,
        grid=(P_PAD // WPREP_TN,),
        in_specs=[pl.BlockSpec((WPREP_TN, DEPTH, D_MODEL), lambda j: (j, 0, 0))],
        out_specs=pl.BlockSpec((DEPTH, WPREP_TN, D_MODEL), lambda j: (0, j, 0)),
        out_shape=jax.ShapeDtypeStruct((DEPTH, P_PAD, D_MODEL), BF16),
        compiler_params=_cparams(("parallel",)),
        name="wprep",
    )(w_in_t)


def _inproj_kernel(x_ref, nw_ref, w_ref, wt_ref, p_ref, pt_ref, h_ref):
    j = pl.program_id(1)

    @pl.when(j == 0)
    def _():
        h = _rms(x_ref[...], nw_ref[...]).astype(BF16)
        h_ref[...] = h
        pt_ref[...] = _dot_nt(h, wt_ref[...])

    p_ref[...] = _dot_nt(h_ref[...], w_ref[...])


def _inproj(x, norm_w, w_in_tb, layer):
    t = x.shape[0]
    tm = min(INPROJ_TM, t)
    grid = (t // tm, P_MAIN // INPROJ_TN)
    return pl.pallas_call(
        _inproj_kernel,
        grid=grid,
        in_specs=[
            pl.BlockSpec((tm, D_MODEL), lambda i, j: (i, 0)),
            pl.BlockSpec((None, 1, D_MODEL), lambda i, j: (layer, 0, 0)),
            pl.BlockSpec((None, INPROJ_TN, D_MODEL), lambda i, j: (layer, j, 0)),
            pl.BlockSpec((None, LANES, D_MODEL), lambda i, j: (layer, P_MAIN // LANES, 0)),
        ],
        out_specs=[
            pl.BlockSpec((tm, INPROJ_TN), lambda i, j: (i, j)),
            pl.BlockSpec((tm, LANES), lambda i, j: (i, 0)),
        ],
        out_shape=[jax.ShapeDtypeStruct((t, P_MAIN), F32),
                   jax.ShapeDtypeStruct((t, LANES), F32)],
        scratch_shapes=[pltpu.VMEM((tm, D_MODEL), BF16)],
        compiler_params=_cparams(("parallel", "arbitrary")),
        name="inproj",
    )(x, norm_w, w_in_tb, w_in_tb)


def _outproj_kernel(x_ref, oa_ref, ob_ref, oc_ref, w_ref, o_ref, wb_ref):
    @pl.when(pl.program_id(0) == 0)
    def _():
        wb_ref[...] = w_ref[...].astype(BF16)

    acc = x_ref[...] + jnp.dot(oa_ref[...], wb_ref[0:A_W, :], preferred_element_type=F32)
    acc = acc + jnp.dot(ob_ref[...], wb_ref[A_W:A_W + B_W, :], preferred_element_type=F32)
    o_ref[...] = acc + jnp.dot(oc_ref[...], wb_ref[A_W + B_W:, :], preferred_element_type=F32)


def _outproj(x, oa, ob, oc, w_out, layer):
    t = x.shape[0]
    tm = min(OUTPROJ_TM, t)
    row = lambda w: pl.BlockSpec((tm, w), lambda i: (i, 0))
    return pl.pallas_call(
        _outproj_kernel,
        grid=(t // tm,),
        in_specs=[row(D_MODEL), row(A_W), row(B_W), row(C_W),
                  pl.BlockSpec((None, D_MODEL, D_MODEL), lambda i: (layer, 0, 0))],
        out_specs=row(D_MODEL),
        out_shape=jax.ShapeDtypeStruct((t, D_MODEL), F32),
        scratch_shapes=[pltpu.VMEM((D_MODEL, D_MODEL), BF16)],
        compiler_params=_cparams(("arbitrary",)),
        name="outproj",
    )(x, oa, ob, oc, w_out)


def _rope_tables(s):
    half = A_QK_DIM // 2
    inv_freq = 1.0 / (ROPE_THETA ** (jnp.arange(half, dtype=F32) / half))
    ang = jnp.arange(s).astype(F32)[:, None] * inv_freq[None, :]
    cos, sin = jnp.cos(ang), jnp.sin(ang)
    cos_t = jnp.concatenate([cos, cos, cos, cos], axis=-1)
    sin_t = jnp.concatenate([-sin, sin, -sin, sin], axis=-1)
    return cos_t, sin_t


def _attn_prep_kernel(q_ref, k_ref, v_ref, cos_ref, sin_ref, qo_ref, k0_ref, k1_ref, vo_ref):
    cos = cos_ref[...]
    sin = sin_ref[...]
    lane = lax.broadcasted_iota(jnp.int32, cos.shape, 1)
    first_half = (lane & (A_QK_DIM - 1)) < (A_QK_DIM // 2)
    comp0 = lane < A_QK_DIM
    half = A_QK_DIM // 2

    def rope(t):
        swapped = jnp.where(first_half, pltpu.roll(t, LANES - half, 1), pltpu.roll(t, half, 1))
        return t * cos + swapped * sin

    for h in range(A_HEADS):
        sl = slice(h * HEAD_W, (h + 1) * HEAD_W)
        qo_ref[:, sl] = (rope(q_ref[:, sl]) * (A_QK_DIM ** -0.5)).astype(BF16)
        kr = rope(k_ref[:, sl])
        k0_ref[:, sl] = jnp.where(comp0, kr, 0.0).astype(BF16)
        k1_ref[:, sl] = jnp.where(comp0, 0.0, kr).astype(BF16)
    vo_ref[...] = v_ref[...].astype(BF16)


def _attn_prep(p, cos_t, sin_t, s):
    t = p.shape[0]
    ts = min(PREP_TS, s)
    ns = s // ts
    blk = lambda c: pl.BlockSpec((ts, A_W), lambda i: (i, c))
    tab = pl.BlockSpec((ts, LANES), lambda i: (i % ns, 0))
    out = jax.ShapeDtypeStruct((t, A_W), BF16)
    return pl.pallas_call(
        _attn_prep_kernel,
        grid=(t // ts,),
        in_specs=[blk(0), blk(1), blk(2), tab, tab],
        out_specs=[pl.BlockSpec((ts, A_W), lambda i: (i, 0))] * 4,
        out_shape=[out] * 4,
        compiler_params=_cparams(("parallel",)),
        name="attn_prep",
    )(p, p, p, cos_t, sin_t)


def _attn_kernel(q_ref, k0_ref, k1_ref, v_ref, lam_ref, gain_ref, o_ref,
                 m_ref, l_ref, acc_ref, *, lambda_init, tq, tk):
    qi = pl.program_id(2)
    m_ref[...] = jnp.full(m_ref.shape, -jnp.inf, F32)
    l_ref[...] = jnp.zeros(l_ref.shape, F32)
    acc_ref[...] = jnp.zeros(acc_ref.shape, F32)
    lanes = [slice(hh * HEAD_W, (hh + 1) * HEAD_W) for hh in range(ATT_HPS)]
    chains = [(hh, k_ref) for hh in range(ATT_HPS) for k_ref in (k0_ref, k1_ref)]
    ids = range(len(chains))

    def tile(kt, masked, q_lo=0):
        ks = pl.ds(pl.multiple_of(kt * tk, tk), tk)
        qs = slice(q_lo, tq)
        nq = tq - q_lo
        s = [_dot_nt(k_ref[ks, lanes[hh]], q_ref[qs, lanes[hh]]) for hh, k_ref in chains]
        if masked:
            kidx = kt * tk + lax.broadcasted_iota(jnp.int32, (tk, nq), 0)
            qidx = qi * tq + q_lo + lax.broadcasted_iota(jnp.int32, (tk, nq), 1)
            s = [jnp.where(kidx <= qidx, s[c], -jnp.inf) for c in ids]
        m_old = [m_ref[c, :, qs] for c in ids]
        m_new = [jnp.maximum(m_old[c], jnp.max(s[c], axis=0, keepdims=True)) for c in ids]
        alpha = [jnp.exp(m_old[c] - m_new[c]) for c in ids]
        p = [jnp.exp(s[c] - m_new[c]) for c in ids]
        for c in ids:
            l_ref[c, :, qs] = alpha[c] * l_ref[c, :, qs] + jnp.sum(p[c], axis=0, keepdims=True)
            m_ref[c, :, qs] = m_new[c]
        pv = [_dot_tn(v_ref[ks, lanes[chains[c][0]]], p[c]) for c in ids]
        for c in ids:
            acc_ref[c, :, qs] = alpha[c] * acc_ref[c, :, qs] + pv[c]

    n_full = qi * (tq // tk)

    def body(kt, carry):
        tile(kt, False)
        return carry

    lax.fori_loop(0, n_full, body, 0)
    for d in range(tq // tk):
        tile(n_full + d, True, q_lo=d * tk)

    lp = lam_ref[...]
    lam = (jnp.exp(jnp.sum(lp[0:1] * lp[1:2], axis=-1, keepdims=True))
           - jnp.exp(jnp.sum(lp[2:3] * lp[3:4], axis=-1, keepdims=True)) + lambda_init)
    for hh in range(ATT_HPS):
        c0, c1 = 2 * hh, 2 * hh + 1
        o_t = acc_ref[c0] / l_ref[c0] - lam * (acc_ref[c1] / l_ref[c1])
        o_ref[:, lanes[hh]] = (_rms(o_t.T, gain_ref[...]) * (1.0 - lambda_init)).astype(BF16)


def _attn(qr, k0, k1, vb, lam_p, gain, layer, b, s):
    tq = min(ATT_TQ, s)
    tk = min(ATT_TK, tq)
    nq = s // tq
    lambda_init = 0.8 - 0.6 * math.exp(-0.3 * layer)
    qmap = lambda bi, h, qi: (bi * nq + qi, h)
    kmap = lambda bi, h, qi: (bi, h)
    hw = ATT_HPS * HEAD_W
    nchain = 2 * ATT_HPS
    assert A_HEADS % ATT_HPS == 0
    return pl.pallas_call(
        functools.partial(_attn_kernel, lambda_init=lambda_init, tq=tq, tk=tk),
        grid=(b, A_HEADS // ATT_HPS, nq),
        in_specs=[
            pl.BlockSpec((tq, hw), qmap),
            pl.BlockSpec((s, hw), kmap),
            pl.BlockSpec((s, hw), kmap),
            pl.BlockSpec((s, hw), kmap),
            pl.BlockSpec((None, 4, A_QK_DIM), lambda bi, h, qi: (layer, 0, 0)),
            pl.BlockSpec((None, 1, A_V_DIM), lambda bi, h, qi: (layer, 0, 0)),
        ],
        out_specs=pl.BlockSpec((tq, hw), qmap),
        out_shape=jax.ShapeDtypeStruct((b * s, A_W), BF16),
        scratch_shapes=[pltpu.VMEM((nchain, 1, tq), F32), pltpu.VMEM((nchain, 1, tq), F32),
                        pltpu.VMEM((nchain, A_V_DIM, tq), F32)],
        compiler_params=_cparams(("parallel", "parallel", "arbitrary")),
        name="diff_attn",
    )(qr, k0, k1, vb, lam_p, gain)


def _group_row(x, group, row):
    n = x.shape[0]
    xg = x.reshape(n // group, group, x.shape[1])
    return jnp.broadcast_to(xg[:, row:row + 1, :], xg.shape).reshape(x.shape)


def _hgrn_kernel(q_ref, f_ref, i_ref, g_ref, lbp_ref, gain_ref, o_ref, st_ref, *, layer):
    c = pl.program_id(2)
    n = HGRN_C

    @pl.when(c == 0)
    def _():
        st_ref[...] = jnp.zeros(st_ref.shape, F32)

    lp = lbp_ref[...]
    e = jnp.exp(lp - jnp.max(lp, axis=0, keepdims=True))
    sm = e / jnp.sum(e, axis=0, keepdims=True)
    lb_all = jnp.zeros((1, HGRN_HPS * HEAD_W), F32)
    for r in range(1, layer + 1):
        lb_all = lb_all + sm[r:r + 1]

    row = lax.broadcasted_iota(jnp.int32, (n, n), 0)
    col = lax.broadcasted_iota(jnp.int32, (n, n), 1)
    tok = lax.broadcasted_iota(jnp.int32, (n, HEAD_W), 0)
    tril = (row >= col).astype(BF16)
    gain = gain_ref[...]
    lanes = [slice(hh * HEAD_W, (hh + 1) * HEAD_W) for hh in range(HGRN_HPS)]

    hs = []
    for hh, ln in enumerate(lanes):
        lb = lb_all[:, ln]
        fg = lb + (1.0 - lb) * jax.nn.sigmoid(f_ref[:, ln])
        hs.append(dict(fg=fg, kk=1.0 - fg, qs=_silu(q_ref[:, ln]), v=i_ref[:, ln],
                       st=st_ref[hh]))
    for h in hs:
        h["b"] = _dot_left01(tril, jnp.log(h["fg"]))
    for h in hs:
        h["o"] = _dot_nt(h["qs"] * jnp.exp(h["b"]), h["st"])
        h["a"] = jnp.zeros((n, n), F32)

    half = n // 2
    while half >= HGRN_SUB:
        upper = (tok & half) != 0
        same = (row & -(2 * half)) == (col & -(2 * half))
        for h in hs:
            d = h["b"] - _group_row(h["b"], 2 * half, half - 1)
            e = jnp.exp(jnp.where(upper, d, -d))
            qh = jnp.where(upper, h["qs"] * e, 0.0)
            kh = jnp.where(upper, 0.0, h["kk"] * e)
            h["a"] = h["a"] + jnp.where(same, _dot_nt(qh, kh), 0.0)
        half //= 2
    for h in hs:
        h["o"] = h["o"] + _dot(h["a"], h["v"])

    tmod = tok & (HGRN_SUB - 1)
    for h in hs:
        h["o"] = h["o"] + jnp.sum(h["qs"] * h["kk"], axis=-1, keepdims=True) * h["v"]
    for r in range(1, HGRN_SUB):
        valid = tmod >= r
        for h in hs:
            kr, br, vr = (pltpu.roll(h[name], r, 0) for name in ("kk", "b", "v"))
            z = jnp.where(valid, h["qs"] * kr * jnp.exp(jnp.where(valid, h["b"] - br, 0.0)), 0.0)
            h["o"] = h["o"] + jnp.sum(z, axis=-1, keepdims=True) * vr

    for hh, (h, ln) in enumerate(zip(hs, lanes)):
        o_ref[:, ln] = (_rms(h["o"], gain) * _silu(g_ref[:, ln])).astype(BF16)
        b_last = h["b"][n - 1:n, :]
        kd = h["kk"] * jnp.exp(b_last - h["b"])
        st_ref[hh] = h["st"] * jnp.exp(b_last) + _dot_tn(h["v"], kd)


def _hgrn(p, lb_param, gain, layer, b, s):
    nc = s // HGRN_C
    hw = HGRN_HPS * HEAD_W
    ngrp = B_HEADS // HGRN_HPS
    assert B_HEADS % HGRN_HPS == 0 and (3 * A_W) % hw == 0
    col0 = 3 * A_W // hw
    blk = lambda off: pl.BlockSpec((HGRN_C, hw),
                                   lambda bi, h, c: (bi * nc + c, col0 + off * ngrp + h))
    return pl.pallas_call(
        functools.partial(_hgrn_kernel, layer=layer),
        grid=(b, ngrp, nc),
        in_specs=[blk(0), blk(1), blk(2), blk(3),
                  pl.BlockSpec((DEPTH, hw), lambda bi, h, c: (0, h)),
                  pl.BlockSpec((None, 1, HEAD_W), lambda bi, h, c: (layer, 0, 0))],
        out_specs=pl.BlockSpec((HGRN_C, hw), lambda bi, h, c: (bi * nc + c, h)),
        out_shape=jax.ShapeDtypeStruct((b * s, B_W), BF16),
        scratch_shapes=[pltpu.VMEM((HGRN_HPS, HEAD_W, HEAD_W), F32)],
        compiler_params=_cparams(("parallel", "parallel", "arbitrary")),
        name="hgrn2",
    )(p, p, p, p, lb_param, gain)


def _lane_pick(x, idx):
    lane = lax.broadcasted_iota(jnp.int32, x.shape, 1)
    return jnp.sum(jnp.where(lane == idx, x, 0.0), axis=-1, keepdims=True)


def _gdn_kernel(q_ref, k_ref, v_ref, z_ref, pt_ref, cq_ref, ck_ref, cv_ref, alog_ref, dtb_ref,
                gain_ref, o_ref, xe_ref, s_ref):
    hg = pl.program_id(1)
    c = pl.program_id(2)
    nb, cs, cb = GDN_NB, GDN_C, GDN_CB
    halo = SUBLANES

    @pl.when(c == 0)
    def _():
        s_ref[...] = jnp.zeros(s_ref.shape, F32)
        xe_ref[:, 0:halo, :] = jnp.zeros((3 * GDN_HPS, halo, HEAD_W), F32)

    row = lax.broadcasted_iota(jnp.int32, (cb, cb), 0)
    col = lax.broadcasted_iota(jnp.int32, (cb, cb), 1)
    same = (row & -cs) == (col & -cs)
    incl = jnp.logical_and(same, row >= col)
    strict = jnp.logical_and(same, row > col)
    eye = (row == col).astype(F32)
    tril01 = incl.astype(BF16)
    pt = pt_ref[...]
    alog_row = alog_ref[...]
    dtb_row = dtb_ref[...]
    gain = gain_ref[...]

    def l2n(x):
        return x * lax.rsqrt(jnp.sum(x * x, axis=-1, keepdims=True) + EPS)

    heads = range(GDN_HPS)
    lanes = [slice(hh * HEAD_W, (hh + 1) * HEAD_W) for hh in heads]

    def conv(hh, idx, x_ref, w_ref):
        slot = 3 * hh + idx
        xe_ref[slot, halo:halo + cb, :] = x_ref[:, lanes[hh]]
        w = w_ref[:, lanes[hh]]
        y = jnp.zeros((cb, HEAD_W), F32)
        for j in range(CONV_K):
            off = halo - (CONV_K - 1) + j
            y = y + w[j:j + 1, :] * xe_ref[slot, off:off + cb, :]
        xe_ref[slot, 0:halo, :] = xe_ref[slot, cb:cb + halo, :]
        return _silu(y)

    hs = []
    for hh in heads:
        hd = hg * GDN_HPS + hh
        q = l2n(conv(hh, 0, q_ref, cq_ref)) * (HEAD_W ** -0.5)
        k = l2n(conv(hh, 1, k_ref, ck_ref))
        v = conv(hh, 2, v_ref, cv_ref)
        beta = jax.nn.sigmoid(_lane_pick(pt, hd))
        xg = _lane_pick(pt, C_HEADS + hd) + _lane_pick(dtb_row, hd)
        softplus = jnp.maximum(xg, 0.0) + jnp.log(1.0 + jnp.exp(-jnp.abs(xg)))
        g = -jnp.exp(_lane_pick(alog_row, hd)) * softplus
        hs.append(dict(q=q, k=k, kb=k * beta, vb=v * beta, g=g))

    for h in hs:
        h["bc"] = _dot_left01(tril01, jnp.broadcast_to(h["g"], (cb, HEAD_W)))
    for h in hs:
        bc2 = jnp.concatenate([h["bc"], h["bc"]], axis=1)
        bc_row = h["bc"].T[0:1, :]
        h["decay"] = jnp.where(incl, jnp.exp(jnp.where(incl, bc2 - bc_row, 0.0)), 0.0)
        h["ebc"] = jnp.exp(h["bc"])
    for h in hs:
        h["lmat"] = jnp.where(strict, _dot_nt(h["kb"], h["k"]) * h["decay"], 0.0)
        h["attn"] = _dot_nt(h["q"], h["k"]) * h["decay"]

    for h in hs:
        h["x"] = eye - h["lmat"]
        lh, ll = _split(h["lmat"])
        h["pw"] = _dot3s(lh, ll, lh, ll)
    size = 2
    while True:
        size *= 2
        for h in hs:
            xh, xl = _split(h["x"])
            ph, pl_ = _split(h["pw"])
            if size >= cs:
                h["x"] = h["x"] + _dot3s(xh, xl, ph, pl_)
            else:
                both = _dot3s(jnp.concatenate([xh, ph], axis=0),
                              jnp.concatenate([xl, pl_], axis=0), ph, pl_)
                h["x"] = h["x"] + both[:cb]
                h["pw"] = both[cb:]
        if size >= cs:
            break
    for h in hs:
        uw = _dot3(h["x"], jnp.concatenate([h["vb"], h["kb"] * h["ebc"]], axis=1))
        h["u"] = uw[:, :HEAD_W]
        h["w"] = uw[:, HEAD_W:]
        h["qe"] = h["q"] * h["ebc"]
        h["v_news"], h["o_inter"] = [], []

    for hh, h in enumerate(hs):
        h["s"] = s_ref[hh]
    for n in range(nb):
        sl = slice(n * cs, (n + 1) * cs)
        for h in hs:
            bc = h["bc"]
            b_last = bc[(n + 1) * cs - 1:(n + 1) * cs, :]
            wq = _dot(jnp.concatenate([h["w"][sl], h["qe"][sl]], axis=0), h["s"])
            v_new = h["u"][sl] - wq[:cs]
            h["o_inter"].append(wq[cs:])
            h["v_news"].append(v_new)
            h["s"] = (h["s"] * jnp.exp(b_last)
                      + _dot_tn(h["k"][sl] * jnp.exp(b_last - bc[sl]), v_new))
    for hh, h in enumerate(hs):
        s_ref[hh] = h["s"]
        o = (jnp.concatenate(h["o_inter"], axis=0)
             + _dot(h["attn"], jnp.concatenate(h["v_news"], axis=0)))
        o_ref[:, lanes[hh]] = (_rms(o, gain) * _silu(z_ref[:, lanes[hh]])).astype(BF16)


def _gdn(p, p_tail, conv_w, a_log, dt_bias, gain, layer, b, s):
    cb = GDN_CB
    assert cb == 2 * HEAD_W and s % cb == 0 and C_HEADS % GDN_HPS == 0
    nc = s // cb
    hw = GDN_HPS * HEAD_W
    col0 = (3 * A_W + 4 * B_W) // hw
    blk = lambda off: pl.BlockSpec(
        (cb, hw), lambda bi, h, c: (bi * nc + c, col0 + off * (C_HEADS // GDN_HPS) + h))
    cw = lambda off: pl.BlockSpec(
        (None, CONV_K, hw), lambda bi, h, c: (layer, 0, off * (C_HEADS // GDN_HPS) + h))
    prow = pl.BlockSpec((None, 1, LANES), lambda bi, h, c: (layer, 0, 0))
    return pl.pallas_call(
        _gdn_kernel,
        grid=(b, C_HEADS // GDN_HPS, nc),
        in_specs=[blk(0), blk(1), blk(2), blk(3),
                  pl.BlockSpec((cb, LANES), lambda bi, h, c: (bi * nc + c, 0)),
                  cw(0), cw(1), cw(2), prow, prow,
                  pl.BlockSpec((None, 1, HEAD_W), lambda bi, h, c: (layer, 0, 0))],
        out_specs=pl.BlockSpec((cb, hw), lambda bi, h, c: (bi * nc + c, h)),
        out_shape=jax.ShapeDtypeStruct((b * s, C_W), BF16),
        scratch_shapes=[pltpu.VMEM((3 * GDN_HPS, SUBLANES + cb, HEAD_W), F32),
                        pltpu.VMEM((GDN_HPS, HEAD_W, HEAD_W), F32)],
        compiler_params=_cparams(("parallel", "parallel", "arbitrary")),
        name="gdn",
    )(p, p, p, p, p_tail, conv_w, conv_w, conv_w, a_log, dt_bias, gain)


def kernel(x, ffn1_norm, ffn1_w_gate, ffn1_w_up, ffn1_w_down, mix_norm, w_in, w_out, lambda_q1, lambda_k1, lambda_q2, lambda_k2, diff_gain, hgrn_lb_param, hgrn_gain, gdn_conv_w, gdn_a_log, gdn_dt_bias, gdn_gain, ffn2_norm, ffn2_w_gate, ffn2_w_up, ffn2_w_down, final_norm):
    b, s, d = x.shape
    depth = w_in.shape[0]
    xt = x.reshape(b * s, d)

    w_in_tb = _wprep(jnp.transpose(w_in, (2, 0, 1)))
    lam_p = jnp.stack([lambda_q1, lambda_k1, lambda_q2, lambda_k2], axis=1)
    rows = lambda t: t[:, None, :]
    pad_heads = lambda t: rows(jnp.pad(t, ((0, 0), (0, LANES - t.shape[1]))))
    a_log_p = pad_heads(gdn_a_log)
    dt_bias_p = pad_heads(gdn_dt_bias)
    ffn1_norm, mix_norm, ffn2_norm = rows(ffn1_norm), rows(mix_norm), rows(ffn2_norm)
    diff_g, hgrn_g, gdn_g = rows(diff_gain), rows(hgrn_gain), rows(gdn_gain)
    final_w = final_norm.reshape(1, d)
    cos_t, sin_t = _rope_tables(s)

    for l in range(depth):
        xt = _ffn(xt, ffn1_norm, ffn1_w_gate, ffn1_w_up, ffn1_w_down, final_w, l, False)
        p, p_tail = _inproj(xt, mix_norm, w_in_tb, l)
        qr, k0, k1, vb = _attn_prep(p, cos_t, sin_t, s)
        oa = _attn(qr, k0, k1, vb, lam_p, diff_g, l, b, s)
        ob = _hgrn(p, hgrn_lb_param, hgrn_g, l, b, s)
        oc = _gdn(p, p_tail, gdn_conv_w, a_log_p, dt_bias_p, gdn_g, l, b, s)
        xt = _outproj(xt, oa, ob, oc, w_out, l)
        xt = _ffn(xt, ffn2_norm, ffn2_w_gate, ffn2_w_up, ffn2_w_down, final_w, l, l == depth - 1)
    return xt.reshape(b, s, d)
```

```python
import functools
import math

import jax
import jax.numpy as jnp
from jax import lax
from jax.experimental import pallas as pl
from jax.experimental.pallas import tpu as pltpu

F32 = jnp.float32
BF16 = jnp.bfloat16
HIGHEST = lax.Precision.HIGHEST

D_MODEL = 2048
DEPTH = 4
A_HEADS = 4
A_QK_DIM = 64
A_V_DIM = 128
ROPE_THETA = 10000.0
B_HEADS = 6
C_HEADS = 6
HEAD_W = 128
CONV_K = 4
D_FF = 5632
EPS = 1e-6
A_W = A_HEADS * HEAD_W
B_W = B_HEADS * HEAD_W
C_W = C_HEADS * HEAD_W
P_MAIN = 3 * A_W + 4 * B_W + 4 * C_W
P_TAIL = 2 * C_HEADS
P_PAD = P_MAIN + 128

LANES = 128
SUBLANES = 8
VMEM_LIMIT_BYTES = 56 * 1024 * 1024

FFN_TM = 1024
FFN_TF = 256
INPROJ_TM = 1024
INPROJ_TN = 1536
WPREP_TN = 128
OUTPROJ_TM = 256
PREP_TS = 512
ATT_TQ = 1024
ATT_TK = 512
ATT_HPS = 2
HGRN_C = 128
HGRN_SUB = 2
HGRN_HPS = 6
GDN_C = 64
GDN_NB = 4
GDN_CB = GDN_C * GDN_NB
GDN_HPS = 6
GDN_GROUPS = 1
GDN_SKEW = 0


def _cparams(sem):
    return pltpu.CompilerParams(dimension_semantics=sem, vmem_limit_bytes=VMEM_LIMIT_BYTES)


def _rms(x, w):
    return x * lax.rsqrt(jnp.mean(x * x, axis=-1, keepdims=True) + EPS) * w


def _silu(x):
    return x * jax.nn.sigmoid(x)


def _dot(a, b):
    return jnp.dot(a.astype(BF16), b.astype(BF16), preferred_element_type=F32)


def _dot_nt(a, b):
    return lax.dot_general(a.astype(BF16), b.astype(BF16), (((1,), (1,)), ((), ())),
                           preferred_element_type=F32)


def _dot_tn(a, b):
    return lax.dot_general(a.astype(BF16), b.astype(BF16), (((0,), (0,)), ((), ())),
                           preferred_element_type=F32)


def _split(a):
    hi = a.astype(BF16)
    return hi, (a - hi.astype(F32)).astype(BF16)


def _mm(a, b):
    return jnp.dot(a, b, preferred_element_type=F32)


def _dot3s(ah, al, bh, bl):
    return _mm(jnp.concatenate([ah, ah, al], axis=1), jnp.concatenate([bh, bl, bh], axis=0))


def _dot3(a, b):
    return _dot3s(*_split(a), *_split(b))


def _dot_left01(a_bf16, x):
    x0 = x.astype(BF16)
    r = x - x0.astype(F32)
    x1 = r.astype(BF16)
    x2 = (r - x1.astype(F32)).astype(BF16)
    return _mm(a_bf16, x0) + (_mm(a_bf16, x1) + _mm(a_bf16, x2))


def _ffn_kernel(x_ref, nw_ref, wg_ref, wu_ref, wd_ref, fw_ref, o_ref, h_ref, *, final_norm):
    j = pl.program_id(1)

    @pl.when(j == 0)
    def _():
        x = x_ref[...]
        h_ref[...] = _rms(x, nw_ref[...]).astype(BF16)
        o_ref[...] = x

    h = h_ref[...]
    g = jnp.dot(h, wg_ref[...].astype(BF16), preferred_element_type=F32)
    u = jnp.dot(h, wu_ref[...].astype(BF16), preferred_element_type=F32)
    a = (0.5 * _silu(g) * u).astype(BF16)
    o_ref[...] += jnp.dot(a, wd_ref[...].astype(BF16), preferred_element_type=F32)

    if final_norm:
        @pl.when(j == pl.num_programs(1) - 1)
        def _():
            o_ref[...] = _rms(o_ref[...], fw_ref[...])


def _ffn(x, norm_w, w_gate, w_up, w_down, final_w, layer, final_norm):
    t = x.shape[0]
    tm = min(FFN_TM, t)
    grid = (t // tm, D_FF // FFN_TF)
    return pl.pallas_call(
        functools.partial(_ffn_kernel, final_norm=final_norm),
        grid=grid,
        in_specs=[
            pl.BlockSpec((tm, D_MODEL), lambda i, j: (i, 0)),
            pl.BlockSpec((None, 1, D_MODEL), lambda i, j: (layer, 0, 0)),
            pl.BlockSpec((None, D_MODEL, FFN_TF), lambda i, j: (layer, 0, j)),
            pl.BlockSpec((None, D_MODEL, FFN_TF), lambda i, j: (layer, 0, j)),
            pl.BlockSpec((None, FFN_TF, D_MODEL), lambda i, j: (layer, j, 0)),
            pl.BlockSpec((1, D_MODEL), lambda i, j: (0, 0)),
        ],
        out_specs=pl.BlockSpec((tm, D_MODEL), lambda i, j: (i, 0)),
        out_shape=jax.ShapeDtypeStruct((t, D_MODEL), F32),
        scratch_shapes=[pltpu.VMEM((tm, D_MODEL), BF16)],
        compiler_params=_cparams(("parallel", "arbitrary")),
        name="ffn",
    )(x, norm_w, w_gate, w_up, w_down, final_w)


def _wprep_kernel(w_ref, o_ref):
    j = pl.program_id(0)
    rows = j * WPREP_TN + lax.broadcasted_iota(jnp.int32, (WPREP_TN, D_MODEL), 0)
    for l in range(DEPTH):
        o_ref[l] = jnp.where(rows < P_MAIN + P_TAIL, w_ref[:, l, :], 0.0).astype(BF16)


def _wprep(w_in_t):
    return pl.pallas_call(
        _wprep_kernel,
        grid=(P_PAD // WPREP_TN,),
        in_specs=[pl.BlockSpec((WPREP_TN, DEPTH, D_MODEL), lambda j: (j, 0, 0))],
        out_specs=pl.BlockSpec((DEPTH, WPREP_TN, D_MODEL), lambda j: (0, j, 0)),
        out_shape=jax.ShapeDtypeStruct((DEPTH, P_PAD, D_MODEL), BF16),
        compiler_params=_cparams(("parallel",)),
        name="wprep",
    )(w_in_t)


def _inproj_kernel(x_ref, nw_ref, w_ref, wt_ref, p_ref, pt_ref, h_ref):
    j = pl.program_id(1)

    @pl.when(j == 0)
    def _():
        h = _rms(x_ref[...], nw_ref[...]).astype(BF16)
        h_ref[...] = h
        pt_ref[...] = _dot_nt(h, wt_ref[...])

    p_ref[...] = _dot_nt(h_ref[...], w_ref[...])


def _inproj(x, norm_w, w_in_tb, layer):
    t = x.shape[0]
    tm = min(INPROJ_TM, t)
    grid = (t // tm, P_MAIN // INPROJ_TN)
    return pl.pallas_call(
        _inproj_kernel,
        grid=grid,
        in_specs=[
            pl.BlockSpec((tm, D_MODEL), lambda i, j: (i, 0)),
            pl.BlockSpec((None, 1, D_MODEL), lambda i, j: (layer, 0, 0)),
            pl.BlockSpec((None, INPROJ_TN, D_MODEL), lambda i, j: (layer, j, 0)),
            pl.BlockSpec((None, LANES, D_MODEL), lambda i, j: (layer, P_MAIN // LANES, 0)),
        ],
        out_specs=[
            pl.BlockSpec((tm, INPROJ_TN), lambda i, j: (i, j)),
            pl.BlockSpec((tm, LANES), lambda i, j: (i, 0)),
        ],
        out_shape=[jax.ShapeDtypeStruct((t, P_MAIN), F32),
                   jax.ShapeDtypeStruct((t, LANES), F32)],
        scratch_shapes=[pltpu.VMEM((tm, D_MODEL), BF16)],
        compiler_params=_cparams(("parallel", "arbitrary")),
        name="inproj",
    )(x, norm_w, w_in_tb, w_in_tb)


def _outproj_kernel(x_ref, oa_ref, ob_ref, oc_ref, w_ref, o_ref, wb_ref):
    @pl.when(pl.program_id(0) == 0)
    def _():
        wb_ref[...] = w_ref[...].astype(BF16)

    acc = x_ref[...] + jnp.dot(oa_ref[...], wb_ref[0:A_W, :], preferred_element_type=F32)
    acc = acc + jnp.dot(ob_ref[...], wb_ref[A_W:A_W + B_W, :], preferred_element_type=F32)
    o_ref[...] = acc + jnp.dot(oc_ref[...], wb_ref[A_W + B_W:, :], preferred_element_type=F32)


def _outproj(x, oa, ob, oc, w_out, layer):
    t = x.shape[0]
    tm = min(OUTPROJ_TM, t)
    row = lambda w: pl.BlockSpec((tm, w), lambda i: (i, 0))
    return pl.pallas_call(
        _outproj_kernel,
        grid=(t // tm,),
        in_specs=[row(D_MODEL), row(A_W), row(B_W), row(C_W),
                  pl.BlockSpec((None, D_MODEL, D_MODEL), lambda i: (layer, 0, 0))],
        out_specs=row(D_MODEL),
        out_shape=jax.ShapeDtypeStruct((t, D_MODEL), F32),
        scratch_shapes=[pltpu.VMEM((D_MODEL, D_MODEL), BF16)],
        compiler_params=_cparams(("arbitrary",)),
        name="outproj",
    )(x, oa, ob, oc, w_out)


def _rope_tables(s):
    half = A_QK_DIM // 2
    inv_freq = 1.0 / (ROPE_THETA ** (jnp.arange(half, dtype=F32) / half))
    ang = jnp.arange(s).astype(F32)[:, None] * inv_freq[None, :]
    cos, sin = jnp.cos(ang), jnp.sin(ang)
    cos_t = jnp.concatenate([cos, cos, cos, cos], axis=-1)
    sin_t = jnp.concatenate([-sin, sin, -sin, sin], axis=-1)
    return cos_t, sin_t


def _attn_prep_kernel(q_ref, k_ref, v_ref, cos_ref, sin_ref, qo_ref, k0_ref, k1_ref, vo_ref):
    cos = cos_ref[...]
    sin = sin_ref[...]
    lane = lax.broadcasted_iota(jnp.int32, cos.shape, 1)
    first_half = (lane & (A_QK_DIM - 1)) < (A_QK_DIM // 2)
    comp0 = lane < A_QK_DIM
    half = A_QK_DIM // 2

    def rope(t):
        swapped = jnp.where(first_half, pltpu.roll(t, LANES - half, 1), pltpu.roll(t, half, 1))
        return t * cos + swapped * sin

    for h in range(A_HEADS):
        sl = slice(h * HEAD_W, (h + 1) * HEAD_W)
        qo_ref[:, sl] = (rope(q_ref[:, sl]) * (A_QK_DIM ** -0.5)).astype(BF16)
        kr = rope(k_ref[:, sl])
        k0_ref[:, sl] = jnp.where(comp0, kr, 0.0).astype(BF16)
        k1_ref[:, sl] = jnp.where(comp0, 0.0, kr).astype(BF16)
    vo_ref[...] = v_ref[...].astype(BF16)


def _attn_prep(p, cos_t, sin_t, s):
    t = p.shape[0]
    ts = min(PREP_TS, s)
    ns = s // ts
    blk = lambda c: pl.BlockSpec((ts, A_W), lambda i: (i, c))
    tab = pl.BlockSpec((ts, LANES), lambda i: (i % ns, 0))
    out = jax.ShapeDtypeStruct((t, A_W), BF16)
    return pl.pallas_call(
        _attn_prep_kernel,
        grid=(t // ts,),
        in_specs=[blk(0), blk(1), blk(2), tab, tab],
        out_specs=[pl.BlockSpec((ts, A_W), lambda i: (i, 0))] * 4,
        out_shape=[out] * 4,
        compiler_params=_cparams(("parallel",)),
        name="attn_prep",
    )(p, p, p, cos_t, sin_t)


def _attn_kernel(q_ref, k0_ref, k1_ref, v_ref, lam_ref, gain_ref, o_ref,
                 m_ref, l_ref, acc_ref, s_ref, *, lambda_init, tq, tk):
    qi = pl.program_id(2)
    m_ref[...] = jnp.full(m_ref.shape, -jnp.inf, F32)
    l_ref[...] = jnp.zeros(l_ref.shape, F32)
    acc_ref[...] = jnp.zeros(acc_ref.shape, F32)
    lanes = [slice(hh * HEAD_W, (hh + 1) * HEAD_W) for hh in range(ATT_HPS)]
    chains = [(hh, k_ref) for hh in range(ATT_HPS) for k_ref in (k0_ref, k1_ref)]
    ids = range(len(chains))

    def keys(kt):
        return pl.ds(pl.multiple_of(kt * tk, tk), tk)

    def scores(kt, slot, q_lo=0):
        for c, (hh, k_ref) in enumerate(chains):
            s_ref[slot, c, :, q_lo:] = _dot_nt(k_ref[keys(kt), lanes[hh]], q_ref[q_lo:, lanes[hh]])

    def absorb(kt, slot, masked, q_lo=0, q_hi=tq):
        qs = slice(q_lo, q_hi)
        nq = q_hi - q_lo
        s = [s_ref[slot, c, :, qs] for c in ids]
        if masked:
            kidx = kt * tk + lax.broadcasted_iota(jnp.int32, (tk, nq), 0)
            qidx = qi * tq + q_lo + lax.broadcasted_iota(jnp.int32, (tk, nq), 1)
            s = [jnp.where(kidx <= qidx, s[c], -jnp.inf) for c in ids]
        m_old = [m_ref[c, :, qs] for c in ids]
        m_new = [jnp.maximum(m_old[c], jnp.max(s[c], axis=0, keepdims=True)) for c in ids]
        alpha = [jnp.exp(m_old[c] - m_new[c]) for c in ids]
        p = [jnp.exp(s[c] - m_new[c]) for c in ids]
        for c in ids:
            l_ref[c, :, qs] = alpha[c] * l_ref[c, :, qs] + jnp.sum(p[c], axis=0, keepdims=True)
            m_ref[c, :, qs] = m_new[c]
        pv = [_dot_tn(v_ref[keys(kt), lanes[chains[c][0]]], p[c]) for c in ids]
        for c in ids:
            acc_ref[c, :, qs] = alpha[c] * acc_ref[c, :, qs] + pv[c]

    assert tq == 2 * tk
    n_full = 2 * qi
    scores(0, 0)

    def body(kp, carry):
        kt = 2 * kp
        scores(kt + 1, 1)
        absorb(kt, 0, False)
        scores(kt + 2, 0)
        absorb(kt + 1, 1, False)
        return carry

    lax.fori_loop(0, qi, body, 0)
    scores(n_full + 1, 1, q_lo=tk)
    absorb(n_full, 0, True, 0, tk)
    absorb(n_full, 0, False, tk, tq)
    absorb(n_full + 1, 1, True, tk, tq)

    lp = lam_ref[...]
    lam = (jnp.exp(jnp.sum(lp[0:1] * lp[1:2], axis=-1, keepdims=True))
           - jnp.exp(jnp.sum(lp[2:3] * lp[3:4], axis=-1, keepdims=True)) + lambda_init)
    for hh in range(ATT_HPS):
        c0, c1 = 2 * hh, 2 * hh + 1
        o_t = acc_ref[c0] / l_ref[c0] - lam * (acc_ref[c1] / l_ref[c1])
        o_ref[:, lanes[hh]] = (_rms(o_t.T, gain_ref[...]) * (1.0 - lambda_init)).astype(BF16)


def _attn(qr, k0, k1, vb, lam_p, gain, layer, b, s):
    tq = min(ATT_TQ, s)
    tk = min(ATT_TK, tq)
    nq = s // tq
    lambda_init = 0.8 - 0.6 * math.exp(-0.3 * layer)
    qmap = lambda bi, h, qi: (bi * nq + qi, h)
    kmap = lambda bi, h, qi: (bi, h)
    hw = ATT_HPS * HEAD_W
    nchain = 2 * ATT_HPS
    assert A_HEADS % ATT_HPS == 0
    return pl.pallas_call(
        functools.partial(_attn_kernel, lambda_init=lambda_init, tq=tq, tk=tk),
        grid=(b, A_HEADS // ATT_HPS, nq),
        in_specs=[
            pl.BlockSpec((tq, hw), qmap),
            pl.BlockSpec((s, hw), kmap),
            pl.BlockSpec((s, hw), kmap),
            pl.BlockSpec((s, hw), kmap),
            pl.BlockSpec((None, 4, A_QK_DIM), lambda bi, h, qi: (layer, 0, 0)),
            pl.BlockSpec((None, 1, A_V_DIM), lambda bi, h, qi: (layer, 0, 0)),
        ],
        out_specs=pl.BlockSpec((tq, hw), qmap),
        out_shape=jax.ShapeDtypeStruct((b * s, A_W), BF16),
        scratch_shapes=[pltpu.VMEM((nchain, 1, tq), F32), pltpu.VMEM((nchain, 1, tq), F32),
                        pltpu.VMEM((nchain, A_V_DIM, tq), F32),
                        pltpu.VMEM((2, nchain, tk, tq), F32)],
        compiler_params=_cparams(("parallel", "parallel", "arbitrary")),
        name="diff_attn",
    )(qr, k0, k1, vb, lam_p, gain)


def _group_row(x, group, row):
    n = x.shape[0]
    xg = x.reshape(n // group, group, x.shape[1])
    return jnp.broadcast_to(xg[:, row:row + 1, :], xg.shape).reshape(x.shape)


def _hgrn_kernel(q_ref, f_ref, i_ref, g_ref, lbp_ref, gain_ref, o_ref, st_ref, *, layer):
    c = pl.program_id(2)
    n = HGRN_C

    @pl.when(c == 0)
    def _():
        st_ref[...] = jnp.zeros(st_ref.shape, F32)

    lp = lbp_ref[...]
    e = jnp.exp(lp - jnp.max(lp, axis=0, keepdims=True))
    sm = e / jnp.sum(e, axis=0, keepdims=True)
    lb_all = jnp.zeros((1, HGRN_HPS * HEAD_W), F32)
    for r in range(1, layer + 1):
        lb_all = lb_all + sm[r:r + 1]

    row = lax.broadcasted_iota(jnp.int32, (n, n), 0)
    col = lax.broadcasted_iota(jnp.int32, (n, n), 1)
    tok = lax.broadcasted_iota(jnp.int32, (n, HEAD_W), 0)
    tril = (row >= col).astype(BF16)
    gain = gain_ref[...]
    lanes = [slice(hh * HEAD_W, (hh + 1) * HEAD_W) for hh in range(HGRN_HPS)]

    hs = []
    for hh, ln in enumerate(lanes):
        lb = lb_all[:, ln]
        fg = lb + (1.0 - lb) * jax.nn.sigmoid(f_ref[:, ln])
        hs.append(dict(fg=fg, kk=1.0 - fg, qs=_silu(q_ref[:, ln]), v=i_ref[:, ln],
                       st=st_ref[hh]))
    for h in hs:
        h["b"] = _dot_left01(tril, jnp.log(h["fg"]))
    for h in hs:
        h["o"] = _dot_nt(h["qs"] * jnp.exp(h["b"]), h["st"])
        h["a"] = jnp.zeros((n, n), F32)

    half = n // 2
    while half >= HGRN_SUB:
        upper = (tok & half) != 0
        same = (row & -(2 * half)) == (col & -(2 * half))
        for h in hs:
            d = h["b"] - _group_row(h["b"], 2 * half, half - 1)
            e = jnp.exp(jnp.where(upper, d, -d))
            qh = jnp.where(upper, h["qs"] * e, 0.0)
            kh = jnp.where(upper, 0.0, h["kk"] * e)
            h["a"] = h["a"] + jnp.where(same, _dot_nt(qh, kh), 0.0)
        half //= 2
    for h in hs:
        h["o"] = h["o"] + _dot(h["a"], h["v"])

    tmod = tok & (HGRN_SUB - 1)
    for h in hs:
        h["o"] = h["o"] + jnp.sum(h["qs"] * h["kk"], axis=-1, keepdims=True) * h["v"]
    for r in range(1, HGRN_SUB):
        valid = tmod >= r
        for h in hs:
            kr, br, vr = (pltpu.roll(h[name], r, 0) for name in ("kk", "b", "v"))
            z = jnp.where(valid, h["qs"] * kr * jnp.exp(jnp.where(valid, h["b"] - br, 0.0)), 0.0)
            h["o"] = h["o"] + jnp.sum(z, axis=-1, keepdims=True) * vr

    for hh, (h, ln) in enumerate(zip(hs, lanes)):
        o_ref[:, ln] = (_rms(h["o"], gain) * _silu(g_ref[:, ln])).astype(BF16)
        b_last = h["b"][n - 1:n, :]
        kd = h["kk"] * jnp.exp(b_last - h["b"])
        st_ref[hh] = h["st"] * jnp.exp(b_last) + _dot_tn(h["v"], kd)


def _hgrn(p, lb_param, gain, layer, b, s):
    nc = s // HGRN_C
    hw = HGRN_HPS * HEAD_W
    ngrp = B_HEADS // HGRN_HPS
    assert B_HEADS % HGRN_HPS == 0 and (3 * A_W) % hw == 0
    col0 = 3 * A_W // hw
    blk = lambda off: pl.BlockSpec((HGRN_C, hw),
                                   lambda bi, h, c: (bi * nc + c, col0 + off * ngrp + h))
    return pl.pallas_call(
        functools.partial(_hgrn_kernel, layer=layer),
        grid=(b, ngrp, nc),
        in_specs=[blk(0), blk(1), blk(2), blk(3),
                  pl.BlockSpec((DEPTH, hw), lambda bi, h, c: (0, h)),
                  pl.BlockSpec((None, 1, HEAD_W), lambda bi, h, c: (layer, 0, 0))],
        out_specs=pl.BlockSpec((HGRN_C, hw), lambda bi, h, c: (bi * nc + c, h)),
        out_shape=jax.ShapeDtypeStruct((b * s, B_W), BF16),
        scratch_shapes=[pltpu.VMEM((HGRN_HPS, HEAD_W, HEAD_W), F32)],
        compiler_params=_cparams(("parallel", "parallel", "arbitrary")),
        name="hgrn2",
    )(p, p, p, p, lb_param, gain)


def _lane_pick(x, idx):
    lane = lax.broadcasted_iota(jnp.int32, x.shape, 1)
    return jnp.sum(jnp.where(lane == idx, x, 0.0), axis=-1, keepdims=True)


def _gdn_kernel(q_ref, k_ref, v_ref, z_ref, pt_ref, cq_ref, ck_ref, cv_ref, alog_ref, dtb_ref,
                gain_ref, o_ref, xe_ref, s_ref):
    hg = pl.program_id(1)
    c = pl.program_id(2)
    nb, cs, cb = GDN_NB, GDN_C, GDN_CB
    halo = SUBLANES

    @pl.when(c == 0)
    def _():
        s_ref[...] = jnp.zeros(s_ref.shape, F32)
        xe_ref[:, 0:halo, :] = jnp.zeros((3 * GDN_HPS, halo, HEAD_W), F32)

    row = lax.broadcasted_iota(jnp.int32, (cb, cb), 0)
    col = lax.broadcasted_iota(jnp.int32, (cb, cb), 1)
    same = (row & -cs) == (col & -cs)
    incl = jnp.logical_and(same, row >= col)
    strict = jnp.logical_and(same, row > col)
    eye = (row == col).astype(F32)
    tril01 = incl.astype(BF16)
    pt = pt_ref[...]
    alog_row = alog_ref[...]
    dtb_row = dtb_ref[...]
    gain = gain_ref[...]

    def l2n(x):
        return x * lax.rsqrt(jnp.sum(x * x, axis=-1, keepdims=True) + EPS)

    heads = range(GDN_HPS)
    lanes = [slice(hh * HEAD_W, (hh + 1) * HEAD_W) for hh in heads]

    def conv(hh, idx, x_ref, w_ref):
        slot = 3 * hh + idx
        xe_ref[slot, halo:halo + cb, :] = x_ref[:, lanes[hh]]
        w = w_ref[:, lanes[hh]]
        y = jnp.zeros((cb, HEAD_W), F32)
        for j in range(CONV_K):
            off = halo - (CONV_K - 1) + j
            y = y + w[j:j + 1, :] * xe_ref[slot, off:off + cb, :]
        xe_ref[slot, 0:halo, :] = xe_ref[slot, cb:cb + halo, :]
        return _silu(y)

    def st_inputs(h):
        hh = h["hh"]
        hd = hg * GDN_HPS + hh
        h["q"] = l2n(conv(hh, 0, q_ref, cq_ref)) * (HEAD_W ** -0.5)
        h["k"] = l2n(conv(hh, 1, k_ref, ck_ref))
        v = conv(hh, 2, v_ref, cv_ref)
        beta = jax.nn.sigmoid(_lane_pick(pt, hd))
        xg = _lane_pick(pt, C_HEADS + hd) + _lane_pick(dtb_row, hd)
        softplus = jnp.maximum(xg, 0.0) + jnp.log(1.0 + jnp.exp(-jnp.abs(xg)))
        h["g"] = -jnp.exp(_lane_pick(alog_row, hd)) * softplus
        h["kb"] = h["k"] * beta
        h["vb"] = v * beta

    def st_cumdecay(h):
        h["bc"] = _dot_left01(tril01, jnp.broadcast_to(h["g"], (cb, HEAD_W)))

    def st_decay(h):
        bc2 = jnp.concatenate([h["bc"], h["bc"]], axis=1)
        bc_row = h["bc"].T[0:1, :]
        h["decay"] = jnp.where(incl, jnp.exp(jnp.where(incl, bc2 - bc_row, 0.0)), 0.0)
        h["ebc"] = jnp.exp(h["bc"])

    def st_scores(h):
        h["lmat"] = jnp.where(strict, _dot_nt(h["kb"], h["k"]) * h["decay"], 0.0)
        h["attn"] = _dot_nt(h["q"], h["k"]) * h["decay"]

    def st_square(h):
        h["x"] = eye - h["lmat"]
        lh, ll = _split(h["lmat"])
        h["pw"] = _dot3s(lh, ll, lh, ll)

    def st_level(last):
        def run(h):
            xh, xl = _split(h["x"])
            ph, pl_ = _split(h["pw"])
            if last:
                h["x"] = h["x"] + _dot3s(xh, xl, ph, pl_)
            else:
                both = _dot3s(jnp.concatenate([xh, ph], axis=0),
                              jnp.concatenate([xl, pl_], axis=0), ph, pl_)
                h["x"] = h["x"] + both[:cb]
                h["pw"] = both[cb:]
        return run

    def st_solve(h):
        uw = _dot3(h["x"], jnp.concatenate([h["vb"], h["kb"] * h["ebc"]], axis=1))
        h["u"] = uw[:, :HEAD_W]
        h["w"] = uw[:, HEAD_W:]
        h["qe"] = h["q"] * h["ebc"]
        h["v_news"], h["o_inter"] = [], []
        h["s"] = s_ref[h["hh"]]

    def st_chunk(n):
        def run(h):
            sl = slice(n * cs, (n + 1) * cs)
            bc = h["bc"]
            b_last = bc[(n + 1) * cs - 1:(n + 1) * cs, :]
            wq = _dot(jnp.concatenate([h["w"][sl], h["qe"][sl]], axis=0), h["s"])
            v_new = h["u"][sl] - wq[:cs]
            h["o_inter"].append(wq[cs:])
            h["v_news"].append(v_new)
            h["s"] = (h["s"] * jnp.exp(b_last)
                      + _dot_tn(h["k"][sl] * jnp.exp(b_last - bc[sl]), v_new))
        return run

    def st_output(h):
        hh = h["hh"]
        s_ref[hh] = h["s"]
        o = (jnp.concatenate(h["o_inter"], axis=0)
             + _dot(h["attn"], jnp.concatenate(h["v_news"], axis=0)))
        o_ref[:, lanes[hh]] = (_rms(o, gain) * _silu(z_ref[:, lanes[hh]])).astype(BF16)

    n_levels = cs.bit_length() - 2
    stages = ([st_inputs, st_cumdecay, st_decay, st_scores, st_square]
              + [st_level(i == n_levels - 1) for i in range(n_levels)]
              + [st_solve] + [st_chunk(n) for n in range(nb)] + [st_output])
    groups = [[dict(hh=hh) for hh in heads if hh % GDN_GROUPS == gi] for gi in range(GDN_GROUPS)]
    for t in range(len(stages) + GDN_SKEW * (GDN_GROUPS - 1)):
        for gi, grp in enumerate(groups):
            si = t - gi * GDN_SKEW
            if 0 <= si < len(stages):
                for h in grp:
                    stages[si](h)


def _gdn(p, p_tail, conv_w, a_log, dt_bias, gain, layer, b, s):
    cb = GDN_CB
    assert cb == 2 * HEAD_W and s % cb == 0 and C_HEADS % GDN_HPS == 0
    nc = s // cb
    hw = GDN_HPS * HEAD_W
    col0 = (3 * A_W + 4 * B_W) // hw
    blk = lambda off: pl.BlockSpec(
        (cb, hw), lambda bi, h, c: (bi * nc + c, col0 + off * (C_HEADS // GDN_HPS) + h))
    cw = lambda off: pl.BlockSpec(
        (None, CONV_K, hw), lambda bi, h, c: (layer, 0, off * (C_HEADS // GDN_HPS) + h))
    prow = pl.BlockSpec((None, 1, LANES), lambda bi, h, c: (layer, 0, 0))
    return pl.pallas_call(
        _gdn_kernel,
        grid=(b, C_HEADS // GDN_HPS, nc),
        in_specs=[blk(0), blk(1), blk(2), blk(3),
                  pl.BlockSpec((cb, LANES), lambda bi, h, c: (bi * nc + c, 0)),
                  cw(0), cw(1), cw(2), prow, prow,
                  pl.BlockSpec((None, 1, HEAD_W), lambda bi, h, c: (layer, 0, 0))],
        out_specs=pl.BlockSpec((cb, hw), lambda bi, h, c: (bi * nc + c, h)),
        out_shape=jax.ShapeDtypeStruct((b * s, C_W), BF16),
        scratch_shapes=[pltpu.VMEM((3 * GDN_HPS, SUBLANES + cb, HEAD_W), F32),
                        pltpu.VMEM((GDN_HPS, HEAD_W, HEAD_W), F32)],
        compiler_params=_cparams(("parallel", "parallel", "arbitrary")),
        name="gdn",
    )(p, p, p, p, p_tail, conv_w, conv_w, conv_w, a_log, dt_bias, gain)


def kernel(x, ffn1_norm, ffn1_w_gate, ffn1_w_up, ffn1_w_down, mix_norm, w_in, w_out, lambda_q1, lambda_k1, lambda_q2, lambda_k2, diff_gain, hgrn_lb_param, hgrn_gain, gdn_conv_w, gdn_a_log, gdn_dt_bias, gdn_gain, ffn2_norm, ffn2_w_gate, ffn2_w_up, ffn2_w_down, final_norm):
    b, s, d = x.shape
    depth = w_in.shape[0]
    xt = x.reshape(b * s, d)

    w_in_tb = _wprep(jnp.transpose(w_in, (2, 0, 1)))
    lam_p = jnp.stack([lambda_q1, lambda_k1, lambda_q2, lambda_k2], axis=1)
    rows = lambda t: t[:, None, :]
    pad_heads = lambda t: rows(jnp.pad(t, ((0, 0), (0, LANES - t.shape[1]))))
    a_log_p = pad_heads(gdn_a_log)
    dt_bias_p = pad_heads(gdn_dt_bias)
    ffn1_norm, mix_norm, ffn2_norm = rows(ffn1_norm), rows(mix_norm), rows(ffn2_norm)
    diff_g, hgrn_g, gdn_g = rows(diff_gain), rows(hgrn_gain), rows(gdn_gain)
    final_w = final_norm.reshape(1, d)
    cos_t, sin_t = _rope_tables(s)

    for l in range(depth):
        xt = _ffn(xt, ffn1_norm, ffn1_w_gate, ffn1_w_up, ffn1_w_down, final_w, l, False)
        p, p_tail = _inproj(xt, mix_norm, w_in_tb, l)
        qr, k0, k1, vb = _attn_prep(p, cos_t, sin_t, s)
        oa = _attn(qr, k0, k1, vb, lam_p, diff_g, l, b, s)
        ob = _hgrn(p, hgrn_lb_param, hgrn_g, l, b, s)
        oc = _gdn(p, p_tail, gdn_conv_w, a_log_p, dt_bias_p, gdn_g, l, b, s)
        xt = _outproj(xt, oa, ob, oc, w_out, l)
        xt = _ffn(xt, ffn2_norm, ffn2_w_gate, ffn2_w_up, ffn2_w_down, final_w, l, l == depth - 1)
    return xt.reshape(b, s, d)
```

```python
import functools
import math

import jax
import jax.numpy as jnp
from jax import lax
from jax.experimental import pallas as pl
from jax.experimental.pallas import tpu as pltpu

F32 = jnp.float32
BF16 = jnp.bfloat16
HIGHEST = lax.Precision.HIGHEST

D_MODEL = 2048
DEPTH = 4
A_HEADS = 4
A_QK_DIM = 64
A_V_DIM = 128
ROPE_THETA = 10000.0
B_HEADS = 6
C_HEADS = 6
HEAD_W = 128
CONV_K = 4
D_FF = 5632
EPS = 1e-6
A_W = A_HEADS * HEAD_W
B_W = B_HEADS * HEAD_W
C_W = C_HEADS * HEAD_W
P_MAIN = 3 * A_W + 4 * B_W + 4 * C_W
P_TAIL = 2 * C_HEADS
P_PAD = P_MAIN + 128

LANES = 128
SUBLANES = 8
VMEM_LIMIT_BYTES = 60 * 1024 * 1024

FFN_TM = 1024
FFN_TF = 512
INPROJ_TM = 1024
INPROJ_TN = 1920
WPREP_TN = 128
OUTPROJ_TM = 512
PREP_TS = 512
ATT_TQ = 1024
ATT_TK = 512
ATT_HPS = 2
HGRN_C = 128
HGRN_SUB = 2
HGRN_HPS = 6
GDN_C = 64
GDN_NB = 4
GDN_CB = GDN_C * GDN_NB
GDN_HPS = 6
GDN_GROUPS = 1
GDN_SKEW = 0


def _cparams(sem):
    return pltpu.CompilerParams(dimension_semantics=sem, vmem_limit_bytes=VMEM_LIMIT_BYTES)


def _rms(x, w):
    return x * lax.rsqrt(jnp.mean(x * x, axis=-1, keepdims=True) + EPS) * w


def _silu(x):
    return x * jax.nn.sigmoid(x)


def _dot(a, b):
    return jnp.dot(a.astype(BF16), b.astype(BF16), preferred_element_type=F32)


def _dot_nt(a, b):
    return lax.dot_general(a.astype(BF16), b.astype(BF16), (((1,), (1,)), ((), ())),
                           preferred_element_type=F32)


def _dot_tn(a, b):
    return lax.dot_general(a.astype(BF16), b.astype(BF16), (((0,), (0,)), ((), ())),
                           preferred_element_type=F32)


def _split(a):
    hi = a.astype(BF16)
    return hi, (a - hi.astype(F32)).astype(BF16)


def _mm(a, b):
    return jnp.dot(a, b, preferred_element_type=F32)


def _dot3s(ah, al, bh, bl):
    return _mm(jnp.concatenate([ah, ah, al], axis=1), jnp.concatenate([bh, bl, bh], axis=0))


def _dot3(a, b):
    return _dot3s(*_split(a), *_split(b))


def _dot_left01(a_bf16, x):
    x0 = x.astype(BF16)
    r = x - x0.astype(F32)
    x1 = r.astype(BF16)
    x2 = (r - x1.astype(F32)).astype(BF16)
    return _mm(a_bf16, x0) + (_mm(a_bf16, x1) + _mm(a_bf16, x2))


def _ffn_kernel(x_ref, nw_ref, wg_ref, wu_ref, wd_ref, fw_ref, o_ref, h_ref, *, final_norm):
    j = pl.program_id(1)

    @pl.when(j == 0)
    def _():
        x = x_ref[...]
        h_ref[...] = _rms(x, nw_ref[...]).astype(BF16)
        o_ref[...] = x

    h = h_ref[...]
    g = jnp.dot(h, wg_ref[...].astype(BF16), preferred_element_type=F32)
    u = jnp.dot(h, wu_ref[...].astype(BF16), preferred_element_type=F32)
    a = (0.5 * _silu(g) * u).astype(BF16)
    o_ref[...] += jnp.dot(a, wd_ref[...].astype(BF16), preferred_element_type=F32)

    if final_norm:
        @pl.when(j == pl.num_programs(1) - 1)
        def _():
            o_ref[...] = _rms(o_ref[...], fw_ref[...])


def _ffn(x, norm_w, w_gate, w_up, w_down, final_w, layer, final_norm):
    t = x.shape[0]
    tm = min(FFN_TM, t)
    grid = (t // tm, D_FF // FFN_TF)
    return pl.pallas_call(
        functools.partial(_ffn_kernel, final_norm=final_norm),
        grid=grid,
        in_specs=[
            pl.BlockSpec((tm, D_MODEL), lambda i, j: (i, 0), pipeline_mode=pl.Buffered(1)),
            pl.BlockSpec((None, 1, D_MODEL), lambda i, j: (layer, 0, 0)),
            pl.BlockSpec((None, D_MODEL, FFN_TF), lambda i, j: (layer, 0, j)),
            pl.BlockSpec((None, D_MODEL, FFN_TF), lambda i, j: (layer, 0, j)),
            pl.BlockSpec((None, FFN_TF, D_MODEL), lambda i, j: (layer, j, 0)),
            pl.BlockSpec((1, D_MODEL), lambda i, j: (0, 0)),
        ],
        out_specs=pl.BlockSpec((tm, D_MODEL), lambda i, j: (i, 0)),
        out_shape=jax.ShapeDtypeStruct((t, D_MODEL), F32),
        scratch_shapes=[pltpu.VMEM((tm, D_MODEL), BF16)],
        compiler_params=_cparams(("parallel", "arbitrary")),
        name="ffn",
    )(x, norm_w, w_gate, w_up, w_down, final_w)


def _wprep_kernel(w_ref, o_ref):
    j = pl.program_id(0)
    rows = j * WPREP_TN + lax.broadcasted_iota(jnp.int32, (WPREP_TN, D_MODEL), 0)
    for l in range(DEPTH):
        o_ref[l] = jnp.where(rows < P_MAIN + P_TAIL, w_ref[:, l, :], 0.0).astype(BF16)


def _wprep(w_in_t):
    return pl.pallas_call(
        _wprep_kernel,
        grid=(P_PAD // WPREP_TN,),
        in_specs=[pl.BlockSpec((WPREP_TN, DEPTH, D_MODEL), lambda j: (j, 0, 0))],
        out_specs=pl.BlockSpec((DEPTH, WPREP_TN, D_MODEL), lambda j: (0, j, 0)),
        out_shape=jax.ShapeDtypeStruct((DEPTH, P_PAD, D_MODEL), BF16),
        compiler_params=_cparams(("parallel",)),
        name="wprep",
    )(w_in_t)


def _inproj_kernel(x_ref, nw_ref, w_ref, wt_ref, p_ref, pt_ref, h_ref):
    j = pl.program_id(1)

    @pl.when(j == 0)
    def _():
        h = _rms(x_ref[...], nw_ref[...]).astype(BF16)
        h_ref[...] = h
        pt_ref[...] = _dot_nt(h, wt_ref[...])

    p_ref[...] = _dot_nt(h_ref[...], w_ref[...])


def _inproj(x, norm_w, w_in_tb, layer):
    t = x.shape[0]
    tm = min(INPROJ_TM, t)
    grid = (t // tm, P_MAIN // INPROJ_TN)
    return pl.pallas_call(
        _inproj_kernel,
        grid=grid,
        in_specs=[
            pl.BlockSpec((tm, D_MODEL), lambda i, j: (i, 0)),
            pl.BlockSpec((None, 1, D_MODEL), lambda i, j: (layer, 0, 0)),
            pl.BlockSpec((None, INPROJ_TN, D_MODEL), lambda i, j: (layer, j, 0)),
            pl.BlockSpec((None, LANES, D_MODEL), lambda i, j: (layer, P_MAIN // LANES, 0)),
        ],
        out_specs=[
            pl.BlockSpec((tm, INPROJ_TN), lambda i, j: (i, j)),
            pl.BlockSpec((tm, LANES), lambda i, j: (i, 0)),
        ],
        out_shape=[jax.ShapeDtypeStruct((t, P_MAIN), F32),
                   jax.ShapeDtypeStruct((t, LANES), F32)],
        scratch_shapes=[pltpu.VMEM((tm, D_MODEL), BF16)],
        compiler_params=_cparams(("parallel", "arbitrary")),
        name="inproj",
    )(x, norm_w, w_in_tb, w_in_tb)


def _outproj_kernel(x_ref, oa_ref, ob_ref, oc_ref, w_ref, o_ref, wb_ref):
    @pl.when(pl.program_id(0) == 0)
    def _():
        wb_ref[...] = w_ref[...].astype(BF16)

    acc = x_ref[...] + jnp.dot(oa_ref[...], wb_ref[0:A_W, :], preferred_element_type=F32)
    acc = acc + jnp.dot(ob_ref[...], wb_ref[A_W:A_W + B_W, :], preferred_element_type=F32)
    o_ref[...] = acc + jnp.dot(oc_ref[...], wb_ref[A_W + B_W:, :], preferred_element_type=F32)


def _outproj(x, oa, ob, oc, w_out, layer):
    t = x.shape[0]
    tm = min(OUTPROJ_TM, t)
    row = lambda w: pl.BlockSpec((tm, w), lambda i: (i, 0))
    return pl.pallas_call(
        _outproj_kernel,
        grid=(t // tm,),
        in_specs=[row(D_MODEL), row(A_W), row(B_W), row(C_W),
                  pl.BlockSpec((None, D_MODEL, D_MODEL), lambda i: (layer, 0, 0),
                               pipeline_mode=pl.Buffered(1))],
        out_specs=row(D_MODEL),
        out_shape=jax.ShapeDtypeStruct((t, D_MODEL), F32),
        scratch_shapes=[pltpu.VMEM((D_MODEL, D_MODEL), BF16)],
        compiler_params=_cparams(("arbitrary",)),
        name="outproj",
    )(x, oa, ob, oc, w_out)


def _rope_tables(s):
    half = A_QK_DIM // 2
    inv_freq = 1.0 / (ROPE_THETA ** (jnp.arange(half, dtype=F32) / half))
    ang = jnp.arange(s).astype(F32)[:, None] * inv_freq[None, :]
    cos, sin = jnp.cos(ang), jnp.sin(ang)
    cos_t = jnp.concatenate([cos, cos, cos, cos], axis=-1)
    sin_t = jnp.concatenate([-sin, sin, -sin, sin], axis=-1)
    return cos_t, sin_t


def _attn_prep_kernel(q_ref, k_ref, v_ref, cos_ref, sin_ref, qo_ref, k0_ref, k1_ref, vo_ref):
    cos = cos_ref[...]
    sin = sin_ref[...]
    lane = lax.broadcasted_iota(jnp.int32, cos.shape, 1)
    first_half = (lane & (A_QK_DIM - 1)) < (A_QK_DIM // 2)
    comp0 = lane < A_QK_DIM
    half = A_QK_DIM // 2

    def rope(t):
        swapped = jnp.where(first_half, pltpu.roll(t, LANES - half, 1), pltpu.roll(t, half, 1))
        return t * cos + swapped * sin

    for h in range(A_HEADS):
        sl = slice(h * HEAD_W, (h + 1) * HEAD_W)
        qo_ref[:, sl] = (rope(q_ref[:, sl]) * (A_QK_DIM ** -0.5)).astype(BF16)
        kr = rope(k_ref[:, sl])
        k0_ref[:, sl] = jnp.where(comp0, kr, 0.0).astype(BF16)
        k1_ref[:, sl] = jnp.where(comp0, 0.0, kr).astype(BF16)
    vo_ref[...] = v_ref[...].astype(BF16)


def _attn_prep(p, cos_t, sin_t, s):
    t = p.shape[0]
    ts = min(PREP_TS, s)
    ns = s // ts
    blk = lambda c: pl.BlockSpec((ts, A_W), lambda i: (i, c))
    tab = pl.BlockSpec((ts, LANES), lambda i: (i % ns, 0))
    out = jax.ShapeDtypeStruct((t, A_W), BF16)
    return pl.pallas_call(
        _attn_prep_kernel,
        grid=(t // ts,),
        in_specs=[blk(0), blk(1), blk(2), tab, tab],
        out_specs=[pl.BlockSpec((ts, A_W), lambda i: (i, 0))] * 4,
        out_shape=[out] * 4,
        compiler_params=_cparams(("parallel",)),
        name="attn_prep",
    )(p, p, p, cos_t, sin_t)


def _attn_kernel(q_ref, k0_ref, k1_ref, v_ref, lam_ref, gain_ref, o_ref,
                 m_ref, l_ref, acc_ref, s_ref, *, lambda_init, tq, tk):
    qi = pl.program_id(2)
    m_ref[...] = jnp.full(m_ref.shape, -jnp.inf, F32)
    l_ref[...] = jnp.zeros(l_ref.shape, F32)
    acc_ref[...] = jnp.zeros(acc_ref.shape, F32)
    lanes = [slice(hh * HEAD_W, (hh + 1) * HEAD_W) for hh in range(ATT_HPS)]
    chains = [(hh, k_ref) for hh in range(ATT_HPS) for k_ref in (k0_ref, k1_ref)]
    ids = range(len(chains))

    def keys(kt):
        return pl.ds(pl.multiple_of(kt * tk, tk), tk)

    def scores(kt, slot, q_lo=0):
        for c, (hh, k_ref) in enumerate(chains):
            s_ref[slot, c, :, q_lo:] = _dot_nt(k_ref[keys(kt), lanes[hh]], q_ref[q_lo:, lanes[hh]])

    def absorb(kt, slot, masked, q_lo=0, q_hi=tq):
        qs = slice(q_lo, q_hi)
        nq = q_hi - q_lo
        s = [s_ref[slot, c, :, qs] for c in ids]
        if masked:
            kidx = kt * tk + lax.broadcasted_iota(jnp.int32, (tk, nq), 0)
            qidx = qi * tq + q_lo + lax.broadcasted_iota(jnp.int32, (tk, nq), 1)
            s = [jnp.where(kidx <= qidx, s[c], -jnp.inf) for c in ids]
        m_old = [m_ref[c, :, qs] for c in ids]
        m_new = [jnp.maximum(m_old[c], jnp.max(s[c], axis=0, keepdims=True)) for c in ids]
        alpha = [jnp.exp(m_old[c] - m_new[c]) for c in ids]
        p = [jnp.exp(s[c] - m_new[c]) for c in ids]
        for c in ids:
            l_ref[c, :, qs] = alpha[c] * l_ref[c, :, qs] + jnp.sum(p[c], axis=0, keepdims=True)
            m_ref[c, :, qs] = m_new[c]
        pv = [_dot_tn(v_ref[keys(kt), lanes[chains[c][0]]], p[c]) for c in ids]
        for c in ids:
            acc_ref[c, :, qs] = alpha[c] * acc_ref[c, :, qs] + pv[c]

    assert tq == 2 * tk
    n_full = 2 * qi
    scores(0, 0)

    def body(kp, carry):
        kt = 2 * kp
        scores(kt + 1, 1)
        absorb(kt, 0, False)
        scores(kt + 2, 0)
        absorb(kt + 1, 1, False)
        return carry

    lax.fori_loop(0, qi, body, 0)
    scores(n_full + 1, 1, q_lo=tk)
    absorb(n_full, 0, True, 0, tk)
    absorb(n_full, 0, False, tk, tq)
    absorb(n_full + 1, 1, True, tk, tq)

    lp = lam_ref[...]
    lam = (jnp.exp(jnp.sum(lp[0:1] * lp[1:2], axis=-1, keepdims=True))
           - jnp.exp(jnp.sum(lp[2:3] * lp[3:4], axis=-1, keepdims=True)) + lambda_init)
    for hh in range(ATT_HPS):
        c0, c1 = 2 * hh, 2 * hh + 1
        o_t = acc_ref[c0] / l_ref[c0] - lam * (acc_ref[c1] / l_ref[c1])
        o_ref[:, lanes[hh]] = (_rms(o_t.T, gain_ref[...]) * (1.0 - lambda_init)).astype(BF16)


def _attn(qr, k0, k1, vb, lam_p, gain, layer, b, s):
    tq = min(ATT_TQ, s)
    tk = min(ATT_TK, tq)
    nq = s // tq
    lambda_init = 0.8 - 0.6 * math.exp(-0.3 * layer)
    qmap = lambda bi, h, qi: (bi * nq + qi, h)
    kmap = lambda bi, h, qi: (bi, h)
    hw = ATT_HPS * HEAD_W
    nchain = 2 * ATT_HPS
    assert A_HEADS % ATT_HPS == 0
    return pl.pallas_call(
        functools.partial(_attn_kernel, lambda_init=lambda_init, tq=tq, tk=tk),
        grid=(b, A_HEADS // ATT_HPS, nq),
        in_specs=[
            pl.BlockSpec((tq, hw), qmap),
            pl.BlockSpec((s, hw), kmap),
            pl.BlockSpec((s, hw), kmap),
            pl.BlockSpec((s, hw), kmap),
            pl.BlockSpec((None, 4, A_QK_DIM), lambda bi, h, qi: (layer, 0, 0)),
            pl.BlockSpec((None, 1, A_V_DIM), lambda bi, h, qi: (layer, 0, 0)),
        ],
        out_specs=pl.BlockSpec((tq, hw), qmap),
        out_shape=jax.ShapeDtypeStruct((b * s, A_W), BF16),
        scratch_shapes=[pltpu.VMEM((nchain, 1, tq), F32), pltpu.VMEM((nchain, 1, tq), F32),
                        pltpu.VMEM((nchain, A_V_DIM, tq), F32),
                        pltpu.VMEM((2, nchain, tk, tq), F32)],
        compiler_params=_cparams(("parallel", "parallel", "arbitrary")),
        name="diff_attn",
    )(qr, k0, k1, vb, lam_p, gain)


def _group_row(x, group, row):
    n = x.shape[0]
    xg = x.reshape(n // group, group, x.shape[1])
    return jnp.broadcast_to(xg[:, row:row + 1, :], xg.shape).reshape(x.shape)


def _hgrn_kernel(q_ref, f_ref, i_ref, g_ref, lbp_ref, gain_ref, o_ref, st_ref, *, layer):
    c = pl.program_id(2)
    n = HGRN_C

    @pl.when(c == 0)
    def _():
        st_ref[...] = jnp.zeros(st_ref.shape, F32)

    lp = lbp_ref[...]
    e = jnp.exp(lp - jnp.max(lp, axis=0, keepdims=True))
    sm = e / jnp.sum(e, axis=0, keepdims=True)
    lb_all = jnp.zeros((1, HGRN_HPS * HEAD_W), F32)
    for r in range(1, layer + 1):
        lb_all = lb_all + sm[r:r + 1]

    row = lax.broadcasted_iota(jnp.int32, (n, n), 0)
    col = lax.broadcasted_iota(jnp.int32, (n, n), 1)
    tok = lax.broadcasted_iota(jnp.int32, (n, HEAD_W), 0)
    tril = (row >= col).astype(BF16)
    gain = gain_ref[...]
    lanes = [slice(hh * HEAD_W, (hh + 1) * HEAD_W) for hh in range(HGRN_HPS)]

    hs = []
    for hh, ln in enumerate(lanes):
        lb = lb_all[:, ln]
        fg = lb + (1.0 - lb) * jax.nn.sigmoid(f_ref[:, ln])
        hs.append(dict(fg=fg, kk=1.0 - fg, qs=_silu(q_ref[:, ln]), v=i_ref[:, ln],
                       st=st_ref[hh]))
    for h in hs:
        h["b"] = _dot_left01(tril, jnp.log(h["fg"]))
    for h in hs:
        h["o"] = _dot_nt(h["qs"] * jnp.exp(h["b"]), h["st"])
        h["a"] = jnp.zeros((n, n), F32)

    half = n // 2
    while half >= HGRN_SUB:
        upper = (tok & half) != 0
        same = (row & -(2 * half)) == (col & -(2 * half))
        for h in hs:
            d = h["b"] - _group_row(h["b"], 2 * half, half - 1)
            e = jnp.exp(jnp.where(upper, d, -d))
            qh = jnp.where(upper, h["qs"] * e, 0.0)
            kh = jnp.where(upper, 0.0, h["kk"] * e)
            h["a"] = h["a"] + jnp.where(same, _dot_nt(qh, kh), 0.0)
        half //= 2
    for h in hs:
        h["o"] = h["o"] + _dot(h["a"], h["v"])

    tmod = tok & (HGRN_SUB - 1)
    for h in hs:
        h["o"] = h["o"] + jnp.sum(h["qs"] * h["kk"], axis=-1, keepdims=True) * h["v"]
    for r in range(1, HGRN_SUB):
        valid = tmod >= r
        for h in hs:
            kr, br, vr = (pltpu.roll(h[name], r, 0) for name in ("kk", "b", "v"))
            z = jnp.where(valid, h["qs"] * kr * jnp.exp(jnp.where(valid, h["b"] - br, 0.0)), 0.0)
            h["o"] = h["o"] + jnp.sum(z, axis=-1, keepdims=True) * vr

    for hh, (h, ln) in enumerate(zip(hs, lanes)):
        o_ref[:, ln] = (_rms(h["o"], gain) * _silu(g_ref[:, ln])).astype(BF16)
        b_last = h["b"][n - 1:n, :]
        kd = h["kk"] * jnp.exp(b_last - h["b"])
        st_ref[hh] = h["st"] * jnp.exp(b_last) + _dot_tn(h["v"], kd)


def _hgrn(p, lb_param, gain, layer, b, s):
    nc = s // HGRN_C
    hw = HGRN_HPS * HEAD_W
    ngrp = B_HEADS // HGRN_HPS
    assert B_HEADS % HGRN_HPS == 0 and (3 * A_W) % hw == 0
    col0 = 3 * A_W // hw
    blk = lambda off: pl.BlockSpec((HGRN_C, hw),
                                   lambda bi, h, c: (bi * nc + c, col0 + off * ngrp + h))
    return pl.pallas_call(
        functools.partial(_hgrn_kernel, layer=layer),
        grid=(b, ngrp, nc),
        in_specs=[blk(0), blk(1), blk(2), blk(3),
                  pl.BlockSpec((DEPTH, hw), lambda bi, h, c: (0, h)),
                  pl.BlockSpec((None, 1, HEAD_W), lambda bi, h, c: (layer, 0, 0))],
        out_specs=pl.BlockSpec((HGRN_C, hw), lambda bi, h, c: (bi * nc + c, h)),
        out_shape=jax.ShapeDtypeStruct((b * s, B_W), BF16),
        scratch_shapes=[pltpu.VMEM((HGRN_HPS, HEAD_W, HEAD_W), F32)],
        compiler_params=_cparams(("parallel", "parallel", "arbitrary")),
        name="hgrn2",
    )(p, p, p, p, lb_param, gain)


def _lane_pick(x, idx):
    lane = lax.broadcasted_iota(jnp.int32, x.shape, 1)
    return jnp.sum(jnp.where(lane == idx, x, 0.0), axis=-1, keepdims=True)


def _gdn_kernel(q_ref, k_ref, v_ref, z_ref, pt_ref, cq_ref, ck_ref, cv_ref, alog_ref, dtb_ref,
                gain_ref, o_ref, xe_ref, s_ref):
    hg = pl.program_id(1)
    c = pl.program_id(2)
    nb, cs, cb = GDN_NB, GDN_C, GDN_CB
    halo = SUBLANES

    @pl.when(c == 0)
    def _():
        s_ref[...] = jnp.zeros(s_ref.shape, F32)
        xe_ref[:, 0:halo, :] = jnp.zeros((3 * GDN_HPS, halo, HEAD_W), F32)

    row = lax.broadcasted_iota(jnp.int32, (cb, cb), 0)
    col = lax.broadcasted_iota(jnp.int32, (cb, cb), 1)
    same = (row & -cs) == (col & -cs)
    incl = jnp.logical_and(same, row >= col)
    strict = jnp.logical_and(same, row > col)
    eye = (row == col).astype(F32)
    tril01 = incl.astype(BF16)
    pt = pt_ref[...]
    alog_row = alog_ref[...]
    dtb_row = dtb_ref[...]
    gain = gain_ref[...]

    def l2n(x):
        return x * lax.rsqrt(jnp.sum(x * x, axis=-1, keepdims=True) + EPS)

    heads = range(GDN_HPS)
    lanes = [slice(hh * HEAD_W, (hh + 1) * HEAD_W) for hh in heads]

    def conv(hh, idx, x_ref, w_ref):
        slot = 3 * hh + idx
        xe_ref[slot, halo:halo + cb, :] = x_ref[:, lanes[hh]]
        w = w_ref[:, lanes[hh]]
        y = jnp.zeros((cb, HEAD_W), F32)
        for j in range(CONV_K):
            off = halo - (CONV_K - 1) + j
            y = y + w[j:j + 1, :] * xe_ref[slot, off:off + cb, :]
        xe_ref[slot, 0:halo, :] = xe_ref[slot, cb:cb + halo, :]
        return _silu(y)

    def st_inputs(h):
        hh = h["hh"]
        hd = hg * GDN_HPS + hh
        h["q"] = l2n(conv(hh, 0, q_ref, cq_ref)) * (HEAD_W ** -0.5)
        h["k"] = l2n(conv(hh, 1, k_ref, ck_ref))
        v = conv(hh, 2, v_ref, cv_ref)
        beta = jax.nn.sigmoid(_lane_pick(pt, hd))
        xg = _lane_pick(pt, C_HEADS + hd) + _lane_pick(dtb_row, hd)
        softplus = jnp.maximum(xg, 0.0) + jnp.log(1.0 + jnp.exp(-jnp.abs(xg)))
        h["g"] = -jnp.exp(_lane_pick(alog_row, hd)) * softplus
        h["kb"] = h["k"] * beta
        h["vb"] = v * beta

    def st_cumdecay(h):
        h["bc"] = _dot_left01(tril01, jnp.broadcast_to(h["g"], (cb, HEAD_W)))

    def st_decay(h):
        bc2 = jnp.concatenate([h["bc"], h["bc"]], axis=1)
        bc_row = h["bc"].T[0:1, :]
        h["decay"] = jnp.where(incl, jnp.exp(jnp.where(incl, bc2 - bc_row, 0.0)), 0.0)
        h["ebc"] = jnp.exp(h["bc"])

    def st_scores(h):
        h["lmat"] = jnp.where(strict, _dot_nt(h["kb"], h["k"]) * h["decay"], 0.0)
        h["attn"] = _dot_nt(h["q"], h["k"]) * h["decay"]

    def st_square(h):
        h["x"] = eye - h["lmat"]
        lh, ll = _split(h["lmat"])
        h["pw"] = _dot3s(lh, ll, lh, ll)

    def st_level(last):
        def run(h):
            xh, xl = _split(h["x"])
            ph, pl_ = _split(h["pw"])
            if last:
                h["x"] = h["x"] + _dot3s(xh, xl, ph, pl_)
            else:
                both = _dot3s(jnp.concatenate([xh, ph], axis=0),
                              jnp.concatenate([xl, pl_], axis=0), ph, pl_)
                h["x"] = h["x"] + both[:cb]
                h["pw"] = both[cb:]
        return run

    def st_solve(h):
        uw = _dot3(h["x"], jnp.concatenate([h["vb"], h["kb"] * h["ebc"]], axis=1))
        h["u"] = uw[:, :HEAD_W]
        h["w"] = uw[:, HEAD_W:]
        h["qe"] = h["q"] * h["ebc"]
        h["v_news"], h["o_inter"] = [], []
        h["s"] = s_ref[h["hh"]]

    def st_chunk(n):
        def run(h):
            sl = slice(n * cs, (n + 1) * cs)
            bc = h["bc"]
            b_last = bc[(n + 1) * cs - 1:(n + 1) * cs, :]
            wq = _dot(jnp.concatenate([h["w"][sl], h["qe"][sl]], axis=0), h["s"])
            v_new = h["u"][sl] - wq[:cs]
            h["o_inter"].append(wq[cs:])
            h["v_news"].append(v_new)
            h["s"] = (h["s"] * jnp.exp(b_last)
                      + _dot_tn(h["k"][sl] * jnp.exp(b_last - bc[sl]), v_new))
        return run

    def st_output(h):
        hh = h["hh"]
        s_ref[hh] = h["s"]
        o = (jnp.concatenate(h["o_inter"], axis=0)
             + _dot(h["attn"], jnp.concatenate(h["v_news"], axis=0)))
        o_ref[:, lanes[hh]] = (_rms(o, gain) * _silu(z_ref[:, lanes[hh]])).astype(BF16)

    n_levels = cs.bit_length() - 2
    stages = ([st_inputs, st_cumdecay, st_decay, st_scores, st_square]
              + [st_level(i == n_levels - 1) for i in range(n_levels)]
              + [st_solve] + [st_chunk(n) for n in range(nb)] + [st_output])
    groups = [[dict(hh=hh) for hh in heads if hh % GDN_GROUPS == gi] for gi in range(GDN_GROUPS)]
    for t in range(len(stages) + GDN_SKEW * (GDN_GROUPS - 1)):
        for gi, grp in enumerate(groups):
            si = t - gi * GDN_SKEW
            if 0 <= si < len(stages):
                for h in grp:
                    stages[si](h)


def _gdn(p, p_tail, conv_w, a_log, dt_bias, gain, layer, b, s):
    cb = GDN_CB
    assert cb == 2 * HEAD_W and s % cb == 0 and C_HEADS % GDN_HPS == 0
    nc = s // cb
    hw = GDN_HPS * HEAD_W
    col0 = (3 * A_W + 4 * B_W) // hw
    blk = lambda off: pl.BlockSpec(
        (cb, hw), lambda bi, h, c: (bi * nc + c, col0 + off * (C_HEADS // GDN_HPS) + h))
    cw = lambda off: pl.BlockSpec(
        (None, CONV_K, hw), lambda bi, h, c: (layer, 0, off * (C_HEADS // GDN_HPS) + h))
    prow = pl.BlockSpec((None, 1, LANES), lambda bi, h, c: (layer, 0, 0))
    return pl.pallas_call(
        _gdn_kernel,
        grid=(b, C_HEADS // GDN_HPS, nc),
        in_specs=[blk(0), blk(1), blk(2), blk(3),
                  pl.BlockSpec((cb, LANES), lambda bi, h, c: (bi * nc + c, 0)),
                  cw(0), cw(1), cw(2), prow, prow,
                  pl.BlockSpec((None, 1, HEAD_W), lambda bi, h, c: (layer, 0, 0))],
        out_specs=pl.BlockSpec((cb, hw), lambda bi, h, c: (bi * nc + c, h)),
        out_shape=jax.ShapeDtypeStruct((b * s, C_W), BF16),
        scratch_shapes=[pltpu.VMEM((3 * GDN_HPS, SUBLANES + cb, HEAD_W), F32),
                        pltpu.VMEM((GDN_HPS, HEAD_W, HEAD_W), F32)],
        compiler_params=_cparams(("parallel", "parallel", "arbitrary")),
        name="gdn",
    )(p, p, p, p, p_tail, conv_w, conv_w, conv_w, a_log, dt_bias, gain)


def kernel(x, ffn1_norm, ffn1_w_gate, ffn1_w_up, ffn1_w_down, mix_norm, w_in, w_out, lambda_q1, lambda_k1, lambda_q2, lambda_k2, diff_gain, hgrn_lb_param, hgrn_gain, gdn_conv_w, gdn_a_log, gdn_dt_bias, gdn_gain, ffn2_norm, ffn2_w_gate, ffn2_w_up, ffn2_w_down, final_norm):
    b, s, d = x.shape
    depth = w_in.shape[0]
    xt = x.reshape(b * s, d)

    w_in_tb = _wprep(jnp.transpose(w_in, (2, 0, 1)))
    lam_p = jnp.stack([lambda_q1, lambda_k1, lambda_q2, lambda_k2], axis=1)
    rows = lambda t: t[:, None, :]
    pad_heads = lambda t: rows(jnp.pad(t, ((0, 0), (0, LANES - t.shape[1]))))
    a_log_p = pad_heads(gdn_a_log)
    dt_bias_p = pad_heads(gdn_dt_bias)
    ffn1_norm, mix_norm, ffn2_norm = rows(ffn1_norm), rows(mix_norm), rows(ffn2_norm)
    diff_g, hgrn_g, gdn_g = rows(diff_gain), rows(hgrn_gain), rows(gdn_gain)
    final_w = final_norm.reshape(1, d)
    cos_t, sin_t = _rope_tables(s)

    for l in range(depth):
        xt = _ffn(xt, ffn1_norm, ffn1_w_gate, ffn1_w_up, ffn1_w_down, final_w, l, False)
        p, p_tail = _inproj(xt, mix_norm, w_in_tb, l)
        qr, k0, k1, vb = _attn_prep(p, cos_t, sin_t, s)
        oa = _attn(qr, k0, k1, vb, lam_p, diff_g, l, b, s)
        ob = _hgrn(p, hgrn_lb_param, hgrn_g, l, b, s)
        oc = _gdn(p, p_tail, gdn_conv_w, a_log_p, dt_bias_p, gdn_g, l, b, s)
        xt = _outproj(xt, oa, ob, oc, w_out, l)
        xt = _ffn(xt, ffn2_norm, ffn2_w_gate, ffn2_w_up, ffn2_w_down, final_w, l, l == depth - 1)
    return xt.reshape(b, s, d)
```

```python
import functools
import math

import jax
import jax.numpy as jnp
from jax import lax
from jax.experimental import pallas as pl
from jax.experimental.pallas import tpu as pltpu

F32 = jnp.float32
BF16 = jnp.bfloat16
HIGHEST = lax.Precision.HIGHEST

D_MODEL = 2048
DEPTH = 4
A_HEADS = 4
A_QK_DIM = 64
A_V_DIM = 128
ROPE_THETA = 10000.0
B_HEADS = 6
C_HEADS = 6
HEAD_W = 128
CONV_K = 4
D_FF = 5632
EPS = 1e-6
LOG2_E = math.log2(math.e)
A_W = A_HEADS * HEAD_W
B_W = B_HEADS * HEAD_W
C_W = C_HEADS * HEAD_W
P_MAIN = 3 * A_W + 4 * B_W + 4 * C_W
P_TAIL = 2 * C_HEADS
P_PAD = P_MAIN + 128

LANES = 128
SUBLANES = 8
VMEM_LIMIT_BYTES = 60 * 1024 * 1024

FFN_TM = 1024
FFN_TF = 256
INPROJ_TM = 1024
INPROJ_TN = 1536
WPREP_TN = 128
OUTPROJ_TM = 512
PREP_TS = 512
ATT_TQ = 1024
ATT_TK = 512
ATT_HPS = 2
HGRN_C = 128
HGRN_SUB = 2
HGRN_HPS = 6
GDN_C = 64
GDN_NB = 4
GDN_CB = GDN_C * GDN_NB
GDN_HPS = 6
GDN_GROUPS = 1
GDN_SKEW = 0


def _cparams(sem):
    return pltpu.CompilerParams(dimension_semantics=sem, vmem_limit_bytes=VMEM_LIMIT_BYTES)


def _rms(x, w):
    return x * lax.rsqrt(jnp.mean(x * x, axis=-1, keepdims=True) + EPS) * w


def _silu(x):
    return x * jax.nn.sigmoid(x)


def _dot(a, b):
    return jnp.dot(a.astype(BF16), b.astype(BF16), preferred_element_type=F32)


def _dot_nt(a, b):
    return lax.dot_general(a.astype(BF16), b.astype(BF16), (((1,), (1,)), ((), ())),
                           preferred_element_type=F32)


def _dot_tn(a, b):
    return lax.dot_general(a.astype(BF16), b.astype(BF16), (((0,), (0,)), ((), ())),
                           preferred_element_type=F32)


def _split(a):
    hi = a.astype(BF16)
    return hi, (a - hi.astype(F32)).astype(BF16)


def _mm(a, b):
    return jnp.dot(a, b, preferred_element_type=F32)


def _dot2s(ah, al, bh):
    return _mm(jnp.concatenate([ah, al], axis=1), jnp.concatenate([bh, bh], axis=0))


def _dot_left01(a_bf16, x):
    x0 = x.astype(BF16)
    r = x - x0.astype(F32)
    x1 = r.astype(BF16)
    x2 = (r - x1.astype(F32)).astype(BF16)
    return _mm(a_bf16, x0) + (_mm(a_bf16, x1) + _mm(a_bf16, x2))


def _ffn_kernel(x_ref, nw_ref, wg_ref, wu_ref, wd_ref, fw_ref, o_ref, h_ref, *, final_norm):
    j = pl.program_id(1)

    @pl.when(j == 0)
    def _():
        x = x_ref[...]
        h_ref[...] = _rms(x, nw_ref[...]).astype(BF16)
        o_ref[...] = x

    h = h_ref[...]
    g = jnp.dot(h, wg_ref[...].astype(BF16), preferred_element_type=F32)
    u = jnp.dot(h, wu_ref[...].astype(BF16), preferred_element_type=F32)
    a = (0.5 * _silu(g) * u).astype(BF16)
    o_ref[...] += jnp.dot(a, wd_ref[...].astype(BF16), preferred_element_type=F32)

    if final_norm:
        @pl.when(j == pl.num_programs(1) - 1)
        def _():
            o_ref[...] = _rms(o_ref[...], fw_ref[...])


def _ffn(x, norm_w, w_gate, w_up, w_down, final_w, layer, final_norm):
    t = x.shape[0]
    tm = min(FFN_TM, t)
    grid = (t // tm, D_FF // FFN_TF)
    return pl.pallas_call(
        functools.partial(_ffn_kernel, final_norm=final_norm),
        grid=grid,
        in_specs=[
            pl.BlockSpec((tm, D_MODEL), lambda i, j: (i, 0)),
            pl.BlockSpec((None, 1, D_MODEL), lambda i, j: (layer, 0, 0)),
            pl.BlockSpec((None, D_MODEL, FFN_TF), lambda i, j: (layer, 0, j)),
            pl.BlockSpec((None, D_MODEL, FFN_TF), lambda i, j: (layer, 0, j)),
            pl.BlockSpec((None, FFN_TF, D_MODEL), lambda i, j: (layer, j, 0)),
            pl.BlockSpec((1, D_MODEL), lambda i, j: (0, 0)),
        ],
        out_specs=pl.BlockSpec((tm, D_MODEL), lambda i, j: (i, 0)),
        out_shape=jax.ShapeDtypeStruct((t, D_MODEL), F32),
        scratch_shapes=[pltpu.VMEM((tm, D_MODEL), BF16)],
        compiler_params=_cparams(("parallel", "arbitrary")),
        name="ffn",
    )(x, norm_w, w_gate, w_up, w_down, final_w)


def _wprep_kernel(w_ref, o_ref):
    j = pl.program_id(0)
    rows = j * WPREP_TN + lax.broadcasted_iota(jnp.int32, (WPREP_TN, D_MODEL), 0)
    for l in range(DEPTH):
        o_ref[l] = jnp.where(rows < P_MAIN + P_TAIL, w_ref[:, l, :], 0.0).astype(BF16)


def _wprep(w_in_t):
    return pl.pallas_call(
        _wprep_kernel,
        grid=(P_PAD // WPREP_TN,),
        in_specs=[pl.BlockSpec((WPREP_TN, DEPTH, D_MODEL), lambda j: (j, 0, 0))],
        out_specs=pl.BlockSpec((DEPTH, WPREP_TN, D_MODEL), lambda j: (0, j, 0)),
        out_shape=jax.ShapeDtypeStruct((DEPTH, P_PAD, D_MODEL), BF16),
        compiler_params=_cparams(("parallel",)),
        name="wprep",
    )(w_in_t)


def _inproj_kernel(x_ref, nw_ref, w_ref, wt_ref, p_ref, pt_ref, h_ref):
    j = pl.program_id(1)

    @pl.when(j == 0)
    def _():
        h = _rms(x_ref[...], nw_ref[...]).astype(BF16)
        h_ref[...] = h
        pt_ref[...] = _dot_nt(h, wt_ref[...])

    p_ref[...] = _dot_nt(h_ref[...], w_ref[...])


def _inproj(x, norm_w, w_in_tb, layer):
    t = x.shape[0]
    tm = min(INPROJ_TM, t)
    grid = (t // tm, P_MAIN // INPROJ_TN)
    return pl.pallas_call(
        _inproj_kernel,
        grid=grid,
        in_specs=[
            pl.BlockSpec((tm, D_MODEL), lambda i, j: (i, 0)),
            pl.BlockSpec((None, 1, D_MODEL), lambda i, j: (layer, 0, 0)),
            pl.BlockSpec((None, INPROJ_TN, D_MODEL), lambda i, j: (layer, j, 0)),
            pl.BlockSpec((None, LANES, D_MODEL), lambda i, j: (layer, P_MAIN // LANES, 0)),
        ],
        out_specs=[
            pl.BlockSpec((tm, INPROJ_TN), lambda i, j: (i, j)),
            pl.BlockSpec((tm, LANES), lambda i, j: (i, 0)),
        ],
        out_shape=[jax.ShapeDtypeStruct((t, P_MAIN), F32),
                   jax.ShapeDtypeStruct((t, LANES), F32)],
        scratch_shapes=[pltpu.VMEM((tm, D_MODEL), BF16)],
        compiler_params=_cparams(("parallel", "arbitrary")),
        name="inproj",
    )(x, norm_w, w_in_tb, w_in_tb)


def _outproj_kernel(x_ref, oa_ref, ob_ref, oc_ref, w_ref, o_ref, wb_ref):
    @pl.when(pl.program_id(0) == 0)
    def _():
        wb_ref[...] = w_ref[...].astype(BF16)

    acc = x_ref[...] + jnp.dot(oa_ref[...], wb_ref[0:A_W, :], preferred_element_type=F32)
    acc = acc + jnp.dot(ob_ref[...], wb_ref[A_W:A_W + B_W, :], preferred_element_type=F32)
    o_ref[...] = acc + jnp.dot(oc_ref[...], wb_ref[A_W + B_W:, :], preferred_element_type=F32)


def _outproj(x, oa, ob, oc, w_out, layer):
    t = x.shape[0]
    tm = min(OUTPROJ_TM, t)
    row = lambda w: pl.BlockSpec((tm, w), lambda i: (i, 0))
    return pl.pallas_call(
        _outproj_kernel,
        grid=(t // tm,),
        in_specs=[row(D_MODEL), row(A_W), row(B_W), row(C_W),
                  pl.BlockSpec((None, D_MODEL, D_MODEL), lambda i: (layer, 0, 0),
                               pipeline_mode=pl.Buffered(1))],
        out_specs=row(D_MODEL),
        out_shape=jax.ShapeDtypeStruct((t, D_MODEL), F32),
        scratch_shapes=[pltpu.VMEM((D_MODEL, D_MODEL), BF16)],
        compiler_params=_cparams(("arbitrary",)),
        name="outproj",
    )(x, oa, ob, oc, w_out)


def _rope_tables(s):
    half = A_QK_DIM // 2
    inv_freq = 1.0 / (ROPE_THETA ** (jnp.arange(half, dtype=F32) / half))
    ang = jnp.arange(s).astype(F32)[:, None] * inv_freq[None, :]
    cos, sin = jnp.cos(ang), jnp.sin(ang)
    cos_t = jnp.concatenate([cos, cos, cos, cos], axis=-1)
    sin_t = jnp.concatenate([-sin, sin, -sin, sin], axis=-1)
    return cos_t, sin_t


def _attn_prep_kernel(q_ref, k_ref, v_ref, cos_ref, sin_ref, qo_ref, k0_ref, k1_ref, vo_ref):
    cos = cos_ref[...]
    sin = sin_ref[...]
    lane = lax.broadcasted_iota(jnp.int32, cos.shape, 1)
    first_half = (lane & (A_QK_DIM - 1)) < (A_QK_DIM // 2)
    comp0 = lane < A_QK_DIM
    half = A_QK_DIM // 2

    def rope(t):
        swapped = jnp.where(first_half, pltpu.roll(t, LANES - half, 1), pltpu.roll(t, half, 1))
        return t * cos + swapped * sin

    for h in range(A_HEADS):
        sl = slice(h * HEAD_W, (h + 1) * HEAD_W)
        qo_ref[:, sl] = (rope(q_ref[:, sl]) * (A_QK_DIM ** -0.5 * LOG2_E)).astype(BF16)
        kr = rope(k_ref[:, sl])
        k0_ref[:, sl] = jnp.where(comp0, kr, 0.0).astype(BF16)
        k1_ref[:, sl] = jnp.where(comp0, 0.0, kr).astype(BF16)
    vo_ref[...] = v_ref[...].astype(BF16)


def _attn_prep(p, cos_t, sin_t, s):
    t = p.shape[0]
    ts = min(PREP_TS, s)
    ns = s // ts
    blk = lambda c: pl.BlockSpec((ts, A_W), lambda i: (i, c))
    tab = pl.BlockSpec((ts, LANES), lambda i: (i % ns, 0))
    out = jax.ShapeDtypeStruct((t, A_W), BF16)
    return pl.pallas_call(
        _attn_prep_kernel,
        grid=(t // ts,),
        in_specs=[blk(0), blk(1), blk(2), tab, tab],
        out_specs=[pl.BlockSpec((ts, A_W), lambda i: (i, 0))] * 4,
        out_shape=[out] * 4,
        compiler_params=_cparams(("parallel",)),
        name="attn_prep",
    )(p, p, p, cos_t, sin_t)


def _attn_kernel(q_ref, k0_ref, k1_ref, v_ref, lam_ref, gain_ref, o_ref,
                 m_ref, l_ref, acc_ref, s_ref, *, lambda_init, tq, tk):
    qi = pl.program_id(2)
    m_ref[...] = jnp.full(m_ref.shape, -jnp.inf, F32)
    l_ref[...] = jnp.zeros(l_ref.shape, F32)
    acc_ref[...] = jnp.zeros(acc_ref.shape, F32)
    lanes = [slice(hh * HEAD_W, (hh + 1) * HEAD_W) for hh in range(ATT_HPS)]
    chains = [(hh, k_ref) for hh in range(ATT_HPS) for k_ref in (k0_ref, k1_ref)]
    ids = range(len(chains))

    def keys(kt):
        return pl.ds(pl.multiple_of(kt * tk, tk), tk)

    def scores(kt, slot, q_lo=0):
        for c, (hh, k_ref) in enumerate(chains):
            s_ref[slot, c, :, q_lo:] = _dot_nt(k_ref[keys(kt), lanes[hh]], q_ref[q_lo:, lanes[hh]])

    def absorb(kt, slot, masked, q_lo=0, q_hi=tq):
        qs = slice(q_lo, q_hi)
        nq = q_hi - q_lo
        s = [s_ref[slot, c, :, qs] for c in ids]
        if masked:
            kidx = kt * tk + lax.broadcasted_iota(jnp.int32, (tk, nq), 0)
            qidx = qi * tq + q_lo + lax.broadcasted_iota(jnp.int32, (tk, nq), 1)
            s = [jnp.where(kidx <= qidx, s[c], -jnp.inf) for c in ids]
        m_old = [m_ref[c, :, qs] for c in ids]
        m_new = [jnp.maximum(m_old[c], jnp.max(s[c], axis=0, keepdims=True)) for c in ids]
        alpha = [jnp.exp2(m_old[c] - m_new[c]) for c in ids]
        p = [jnp.exp2(s[c] - m_new[c]) for c in ids]
        for c in ids:
            l_ref[c, :, qs] = alpha[c] * l_ref[c, :, qs] + jnp.sum(p[c], axis=0, keepdims=True)
            m_ref[c, :, qs] = m_new[c]
        pv = [_dot_tn(v_ref[keys(kt), lanes[chains[c][0]]], p[c]) for c in ids]
        for c in ids:
            acc_ref[c, :, qs] = alpha[c] * acc_ref[c, :, qs] + pv[c]

    assert tq == 2 * tk
    n_full = 2 * qi
    scores(0, 0)

    def body(kp, carry):
        kt = 2 * kp
        scores(kt + 1, 1)
        absorb(kt, 0, False)
        scores(kt + 2, 0)
        absorb(kt + 1, 1, False)
        return carry

    lax.fori_loop(0, qi, body, 0)
    scores(n_full + 1, 1, q_lo=tk)
    absorb(n_full, 0, True, 0, tk)
    absorb(n_full, 0, False, tk, tq)
    absorb(n_full + 1, 1, True, tk, tq)

    lp = lam_ref[...]
    lam = (jnp.exp(jnp.sum(lp[0:1] * lp[1:2], axis=-1, keepdims=True))
           - jnp.exp(jnp.sum(lp[2:3] * lp[3:4], axis=-1, keepdims=True)) + lambda_init)
    for hh in range(ATT_HPS):
        c0, c1 = 2 * hh, 2 * hh + 1
        o_t = acc_ref[c0] / l_ref[c0] - lam * (acc_ref[c1] / l_ref[c1])
        o_ref[:, lanes[hh]] = (_rms(o_t.T, gain_ref[...]) * (1.0 - lambda_init)).astype(BF16)


def _attn(qr, k0, k1, vb, lam_p, gain, layer, b, s):
    tq = min(ATT_TQ, s)
    tk = min(ATT_TK, tq)
    nq = s // tq
    lambda_init = 0.8 - 0.6 * math.exp(-0.3 * layer)
    qmap = lambda bi, h, qi: (bi * nq + qi, h)
    kmap = lambda bi, h, qi: (bi, h)
    hw = ATT_HPS * HEAD_W
    nchain = 2 * ATT_HPS
    assert A_HEADS % ATT_HPS == 0
    return pl.pallas_call(
        functools.partial(_attn_kernel, lambda_init=lambda_init, tq=tq, tk=tk),
        grid=(b, A_HEADS // ATT_HPS, nq),
        in_specs=[
            pl.BlockSpec((tq, hw), qmap),
            pl.BlockSpec((s, hw), kmap),
            pl.BlockSpec((s, hw), kmap),
            pl.BlockSpec((s, hw), kmap),
            pl.BlockSpec((None, 4, A_QK_DIM), lambda bi, h, qi: (layer, 0, 0)),
            pl.BlockSpec((None, 1, A_V_DIM), lambda bi, h, qi: (layer, 0, 0)),
        ],
        out_specs=pl.BlockSpec((tq, hw), qmap),
        out_shape=jax.ShapeDtypeStruct((b * s, A_W), BF16),
        scratch_shapes=[pltpu.VMEM((nchain, 1, tq), F32), pltpu.VMEM((nchain, 1, tq), F32),
                        pltpu.VMEM((nchain, A_V_DIM, tq), F32),
                        pltpu.VMEM((2, nchain, tk, tq), F32)],
        compiler_params=_cparams(("parallel", "parallel", "arbitrary")),
        name="diff_attn",
    )(qr, k0, k1, vb, lam_p, gain)


def _group_row(x, group, row):
    n = x.shape[0]
    xg = x.reshape(n // group, group, x.shape[1])
    return jnp.broadcast_to(xg[:, row:row + 1, :], xg.shape).reshape(x.shape)


def _hgrn_kernel(q_ref, f_ref, i_ref, g_ref, lbp_ref, gain_ref, o_ref, st_ref, *, layer):
    c = pl.program_id(2)
    n = HGRN_C

    @pl.when(c == 0)
    def _():
        st_ref[...] = jnp.zeros(st_ref.shape, F32)

    lp = lbp_ref[...]
    e = jnp.exp(lp - jnp.max(lp, axis=0, keepdims=True))
    sm = e / jnp.sum(e, axis=0, keepdims=True)
    lb_all = jnp.zeros((1, HGRN_HPS * HEAD_W), F32)
    for r in range(1, layer + 1):
        lb_all = lb_all + sm[r:r + 1]

    row = lax.broadcasted_iota(jnp.int32, (n, n), 0)
    col = lax.broadcasted_iota(jnp.int32, (n, n), 1)
    tok = lax.broadcasted_iota(jnp.int32, (n, HEAD_W), 0)
    tril = (row >= col).astype(BF16)
    gain = gain_ref[...]
    lanes = [slice(hh * HEAD_W, (hh + 1) * HEAD_W) for hh in range(HGRN_HPS)]

    hs = []
    for hh, ln in enumerate(lanes):
        lb = lb_all[:, ln]
        fg = lb + (1.0 - lb) * jax.nn.sigmoid(f_ref[:, ln])
        hs.append(dict(fg=fg, kk=1.0 - fg, qs=_silu(q_ref[:, ln]), v=i_ref[:, ln],
                       st=st_ref[hh]))
    for h in hs:
        h["b"] = _dot_left01(tril, jnp.log(h["fg"]))
    for h in hs:
        h["o"] = _dot_nt(h["qs"] * jnp.exp(h["b"]), h["st"])
        h["a"] = jnp.zeros((n, n), F32)

    half = n // 2
    while half >= HGRN_SUB:
        upper = (tok & half) != 0
        same = (row & -(2 * half)) == (col & -(2 * half))
        for h in hs:
            d = h["b"] - _group_row(h["b"], 2 * half, half - 1)
            e = jnp.exp(jnp.where(upper, d, -d))
            qh = jnp.where(upper, h["qs"] * e, 0.0)
            kh = jnp.where(upper, 0.0, h["kk"] * e)
            h["a"] = h["a"] + jnp.where(same, _dot_nt(qh, kh), 0.0)
        half //= 2
    for h in hs:
        h["o"] = h["o"] + _dot(h["a"], h["v"])

    tmod = tok & (HGRN_SUB - 1)
    for h in hs:
        h["o"] = h["o"] + jnp.sum(h["qs"] * h["kk"], axis=-1, keepdims=True) * h["v"]
    for r in range(1, HGRN_SUB):
        valid = tmod >= r
        for h in hs:
            kr, br, vr = (pltpu.roll(h[name], r, 0) for name in ("kk", "b", "v"))
            z = jnp.where(valid, h["qs"] * kr * jnp.exp(jnp.where(valid, h["b"] - br, 0.0)), 0.0)
            h["o"] = h["o"] + jnp.sum(z, axis=-1, keepdims=True) * vr

    for hh, (h, ln) in enumerate(zip(hs, lanes)):
        o_ref[:, ln] = (_rms(h["o"], gain) * _silu(g_ref[:, ln])).astype(BF16)
        b_last = h["b"][n - 1:n, :]
        kd = h["kk"] * jnp.exp(b_last - h["b"])
        st_ref[hh] = h["st"] * jnp.exp(b_last) + _dot_tn(h["v"], kd)


def _hgrn(p, lb_param, gain, layer, b, s):
    nc = s // HGRN_C
    hw = HGRN_HPS * HEAD_W
    ngrp = B_HEADS // HGRN_HPS
    assert B_HEADS % HGRN_HPS == 0 and (3 * A_W) % hw == 0
    col0 = 3 * A_W // hw
    blk = lambda off: pl.BlockSpec((HGRN_C, hw),
                                   lambda bi, h, c: (bi * nc + c, col0 + off * ngrp + h))
    return pl.pallas_call(
        functools.partial(_hgrn_kernel, layer=layer),
        grid=(b, ngrp, nc),
        in_specs=[blk(0), blk(1), blk(2), blk(3),
                  pl.BlockSpec((DEPTH, hw), lambda bi, h, c: (0, h)),
                  pl.BlockSpec((None, 1, HEAD_W), lambda bi, h, c: (layer, 0, 0))],
        out_specs=pl.BlockSpec((HGRN_C, hw), lambda bi, h, c: (bi * nc + c, h)),
        out_shape=jax.ShapeDtypeStruct((b * s, B_W), BF16),
        scratch_shapes=[pltpu.VMEM((HGRN_HPS, HEAD_W, HEAD_W), F32)],
        compiler_params=_cparams(("parallel", "parallel", "arbitrary")),
        name="hgrn2",
    )(p, p, p, p, lb_param, gain)


def _lane_pick(x, idx):
    lane = lax.broadcasted_iota(jnp.int32, x.shape, 1)
    return jnp.sum(jnp.where(lane == idx, x, 0.0), axis=-1, keepdims=True)


def _gdn_kernel(q_ref, k_ref, v_ref, z_ref, pt_ref, cq_ref, ck_ref, cv_ref, alog_ref, dtb_ref,
                gain_ref, o_ref, xe_ref, s_ref):
    hg = pl.program_id(1)
    c = pl.program_id(2)
    nb, cs, cb = GDN_NB, GDN_C, GDN_CB
    halo = SUBLANES

    @pl.when(c == 0)
    def _():
        s_ref[...] = jnp.zeros(s_ref.shape, F32)
        xe_ref[:, 0:halo, :] = jnp.zeros((3 * GDN_HPS, halo, HEAD_W), F32)

    row = lax.broadcasted_iota(jnp.int32, (cb, cb), 0)
    col = lax.broadcasted_iota(jnp.int32, (cb, cb), 1)
    same = (row & -cs) == (col & -cs)
    incl = jnp.logical_and(same, row >= col)
    strict = jnp.logical_and(same, row > col)
    eye = (row == col).astype(F32)
    tril01 = incl.astype(BF16)
    pt = pt_ref[...]
    alog_row = alog_ref[...]
    dtb_row = dtb_ref[...]
    gain = gain_ref[...]

    def l2n(x):
        return x * lax.rsqrt(jnp.sum(x * x, axis=-1, keepdims=True) + EPS)

    heads = range(GDN_HPS)
    lanes = [slice(hh * HEAD_W, (hh + 1) * HEAD_W) for hh in heads]

    def conv(hh, idx, x_ref, w_ref):
        slot = 3 * hh + idx
        xe_ref[slot, halo:halo + cb, :] = x_ref[:, lanes[hh]]
        w = w_ref[:, lanes[hh]]
        y = jnp.zeros((cb, HEAD_W), F32)
        for j in range(CONV_K):
            off = halo - (CONV_K - 1) + j
            y = y + w[j:j + 1, :] * xe_ref[slot, off:off + cb, :]
        xe_ref[slot, 0:halo, :] = xe_ref[slot, cb:cb + halo, :]
        return _silu(y)

    def st_inputs(h):
        hh = h["hh"]
        hd = hg * GDN_HPS + hh
        h["q"] = l2n(conv(hh, 0, q_ref, cq_ref)) * (HEAD_W ** -0.5)
        h["k"] = l2n(conv(hh, 1, k_ref, ck_ref))
        v = conv(hh, 2, v_ref, cv_ref)
        beta = jax.nn.sigmoid(_lane_pick(pt, hd))
        xg = _lane_pick(pt, C_HEADS + hd) + _lane_pick(dtb_row, hd)
        softplus = jnp.maximum(xg, 0.0) + jnp.log(1.0 + jnp.exp(-jnp.abs(xg)))
        h["g"] = -jnp.exp(_lane_pick(alog_row, hd)) * softplus
        h["kb"] = h["k"] * beta
        h["vb"] = v * beta

    def st_cumdecay(h):
        h["bc"] = _dot_left01(tril01, jnp.broadcast_to(h["g"], (cb, HEAD_W)))

    def st_decay(h):
        bc2 = jnp.concatenate([h["bc"], h["bc"]], axis=1)
        bc_row = h["bc"].T[0:1, :]
        h["decay"] = jnp.where(incl, jnp.exp(jnp.where(incl, bc2 - bc_row, 0.0)), 0.0)
        h["ebc"] = jnp.exp(h["bc"])

    def st_scores(h):
        h["lmat"] = jnp.where(strict, _dot_nt(h["kb"], h["k"]) * h["decay"], 0.0)
        h["attn"] = _dot_nt(h["q"], h["k"]) * h["decay"]

    def st_square(h):
        h["x"] = eye - h["lmat"]
        lh, ll = _split(h["lmat"])
        h["pw"] = _dot2s(lh, ll, lh)

    def st_level(last):
        def run(h):
            xh, xl = _split(h["x"])
            ph, pl_ = _split(h["pw"])
            if last:
                h["x"] = h["x"] + _dot2s(xh, xl, ph)
            else:
                both = _dot2s(jnp.concatenate([xh, ph], axis=0),
                              jnp.concatenate([xl, pl_], axis=0), ph)
                h["x"] = h["x"] + both[:cb]
                h["pw"] = both[cb:]
        return run

    def st_solve(h):
        rhs = jnp.concatenate([h["vb"], h["kb"] * h["ebc"]], axis=1)
        uw = _dot2s(*_split(h["x"]), rhs.astype(BF16))
        h["u"] = uw[:, :HEAD_W]
        h["w"] = uw[:, HEAD_W:]
        h["qe"] = h["q"] * h["ebc"]
        h["v_news"], h["o_inter"] = [], []
        h["s"] = s_ref[h["hh"]]

    def st_chunk(n):
        def run(h):
            sl = slice(n * cs, (n + 1) * cs)
            bc = h["bc"]
            b_last = bc[(n + 1) * cs - 1:(n + 1) * cs, :]
            wq = _dot(jnp.concatenate([h["w"][sl], h["qe"][sl]], axis=0), h["s"])
            v_new = h["u"][sl] - wq[:cs]
            h["o_inter"].append(wq[cs:])
            h["v_news"].append(v_new)
            h["s"] = (h["s"] * jnp.exp(b_last)
                      + _dot_tn(h["k"][sl] * jnp.exp(b_last - bc[sl]), v_new))
        return run

    def st_output(h):
        hh = h["hh"]
        s_ref[hh] = h["s"]
        o = (jnp.concatenate(h["o_inter"], axis=0)
             + _dot(h["attn"], jnp.concatenate(h["v_news"], axis=0)))
        o_ref[:, lanes[hh]] = (_rms(o, gain) * _silu(z_ref[:, lanes[hh]])).astype(BF16)

    n_levels = cs.bit_length() - 2
    stages = ([st_inputs, st_cumdecay, st_decay, st_scores, st_square]
              + [st_level(i == n_levels - 1) for i in range(n_levels)]
              + [st_solve] + [st_chunk(n) for n in range(nb)] + [st_output])
    groups = [[dict(hh=hh) for hh in heads if hh % GDN_GROUPS == gi] for gi in range(GDN_GROUPS)]
    for t in range(len(stages) + GDN_SKEW * (GDN_GROUPS - 1)):
        for gi, grp in enumerate(groups):
            si = t - gi * GDN_SKEW
            if 0 <= si < len(stages):
                for h in grp:
                    stages[si](h)


def _gdn(p, p_tail, conv_w, a_log, dt_bias, gain, layer, b, s):
    cb = GDN_CB
    assert cb == 2 * HEAD_W and s % cb == 0 and C_HEADS % GDN_HPS == 0
    nc = s // cb
    hw = GDN_HPS * HEAD_W
    col0 = (3 * A_W + 4 * B_W) // hw
    blk = lambda off: pl.BlockSpec(
        (cb, hw), lambda bi, h, c: (bi * nc + c, col0 + off * (C_HEADS // GDN_HPS) + h))
    cw = lambda off: pl.BlockSpec(
        (None, CONV_K, hw), lambda bi, h, c: (layer, 0, off * (C_HEADS // GDN_HPS) + h))
    prow = pl.BlockSpec((None, 1, LANES), lambda bi, h, c: (layer, 0, 0))
    return pl.pallas_call(
        _gdn_kernel,
        grid=(b, C_HEADS // GDN_HPS, nc),
        in_specs=[blk(0), blk(1), blk(2), blk(3),
                  pl.BlockSpec((cb, LANES), lambda bi, h, c: (bi * nc + c, 0)),
                  cw(0), cw(1), cw(2), prow, prow,
                  pl.BlockSpec((None, 1, HEAD_W), lambda bi, h, c: (layer, 0, 0))],
        out_specs=pl.BlockSpec((cb, hw), lambda bi, h, c: (bi * nc + c, h)),
        out_shape=jax.ShapeDtypeStruct((b * s, C_W), BF16),
        scratch_shapes=[pltpu.VMEM((3 * GDN_HPS, SUBLANES + cb, HEAD_W), F32),
                        pltpu.VMEM((GDN_HPS, HEAD_W, HEAD_W), F32)],
        compiler_params=_cparams(("parallel", "parallel", "arbitrary")),
        name="gdn",
    )(p, p, p, p, p_tail, conv_w, conv_w, conv_w, a_log, dt_bias, gain)


def kernel(x, ffn1_norm, ffn1_w_gate, ffn1_w_up, ffn1_w_down, mix_norm, w_in, w_out, lambda_q1, lambda_k1, lambda_q2, lambda_k2, diff_gain, hgrn_lb_param, hgrn_gain, gdn_conv_w, gdn_a_log, gdn_dt_bias, gdn_gain, ffn2_norm, ffn2_w_gate, ffn2_w_up, ffn2_w_down, final_norm):
    b, s, d = x.shape
    depth = w_in.shape[0]
    xt = x.reshape(b * s, d)

    w_in_tb = _wprep(jnp.transpose(w_in, (2, 0, 1)))
    lam_p = jnp.stack([lambda_q1, lambda_k1, lambda_q2, lambda_k2], axis=1)
    rows = lambda t: t[:, None, :]
    pad_heads = lambda t: rows(jnp.pad(t, ((0, 0), (0, LANES - t.shape[1]))))
    a_log_p = pad_heads(gdn_a_log)
    dt_bias_p = pad_heads(gdn_dt_bias)
    ffn1_norm, mix_norm, ffn2_norm = rows(ffn1_norm), rows(mix_norm), rows(ffn2_norm)
    diff_g, hgrn_g, gdn_g = rows(diff_gain), rows(hgrn_gain), rows(gdn_gain)
    final_w = final_norm.reshape(1, d)
    cos_t, sin_t = _rope_tables(s)

    for l in range(depth):
        xt = _ffn(xt, ffn1_norm, ffn1_w_gate, ffn1_w_up, ffn1_w_down, final_w, l, False)
        p, p_tail = _inproj(xt, mix_norm, w_in_tb, l)
        qr, k0, k1, vb = _attn_prep(p, cos_t, sin_t, s)
        oa = _attn(qr, k0, k1, vb, lam_p, diff_g, l, b, s)
        ob = _hgrn(p, hgrn_lb_param, hgrn_g, l, b, s)
        oc = _gdn(p, p_tail, gdn_conv_w, a_log_p, dt_bias_p, gdn_g, l, b, s)
        xt = _outproj(xt, oa, ob, oc, w_out, l)
        xt = _ffn(xt, ffn2_norm, ffn2_w_gate, ffn2_w_up, ffn2_w_down, final_w, l, l == depth - 1)
    return xt.reshape(b, s, d)
```

```python
import functools
import math

import jax
import jax.numpy as jnp
from jax import lax
from jax.experimental import pallas as pl
from jax.experimental.pallas import tpu as pltpu

F32 = jnp.float32
BF16 = jnp.bfloat16

D_MODEL = 2048
DEPTH = 4
A_HEADS = 4
A_QK_DIM = 64
A_V_DIM = 128
ROPE_THETA = 10000.0
B_HEADS = 6
C_HEADS = 6
HEAD_W = 128
CONV_K = 4
D_FF = 5632
EPS = 1e-6
LOG2_E = math.log2(math.e)
A_W = A_HEADS * HEAD_W
B_W = B_HEADS * HEAD_W
C_W = C_HEADS * HEAD_W
P_MAIN = 3 * A_W + 4 * B_W + 4 * C_W
P_TAIL = 2 * C_HEADS
P_PAD = P_MAIN + 128

LANES = 128
SUBLANES = 8
VMEM_LIMIT_BYTES = 60 * 1024 * 1024

FFN_TM = 1024
FFN_TF = 256
INPROJ_TM = 1024
INPROJ_TN = 1536
WPREP_TN = 128
OUTPROJ_TM = 512
PREP_TS = 512
ATT_TQ = 1024
ATT_TK = 512
ATT_HPS = 2
HGRN_C = 128
HGRN_SUB = 2
HGRN_HPS = 6
GDN_C = 64
GDN_NB = 4
GDN_CB = GDN_C * GDN_NB
GDN_HPS = 6


def _cparams(sem):
    return pltpu.CompilerParams(dimension_semantics=sem, vmem_limit_bytes=VMEM_LIMIT_BYTES)


def _rms(x, w):
    return x * lax.rsqrt(jnp.mean(x * x, axis=-1, keepdims=True) + EPS) * w


def _silu(x):
    hx = 0.5 * x
    return hx + hx * jnp.tanh(hx)


def _dot(a, b):
    return jnp.dot(a.astype(BF16), b.astype(BF16), preferred_element_type=F32)


def _dot_nt(a, b):
    return lax.dot_general(a.astype(BF16), b.astype(BF16), (((1,), (1,)), ((), ())),
                           preferred_element_type=F32)


def _dot_tn(a, b):
    return lax.dot_general(a.astype(BF16), b.astype(BF16), (((0,), (0,)), ((), ())),
                           preferred_element_type=F32)


def _split(a):
    hi = a.astype(BF16)
    return hi, (a - hi.astype(F32)).astype(BF16)


def _mm(a, b):
    return jnp.dot(a, b, preferred_element_type=F32)


def _dot2s(ah, al, bh):
    return _mm(jnp.concatenate([ah, al], axis=1), jnp.concatenate([bh, bh], axis=0))


def _dot_left01(a_bf16, x):
    x0 = x.astype(BF16)
    r = x - x0.astype(F32)
    x1 = r.astype(BF16)
    x2 = (r - x1.astype(F32)).astype(BF16)
    return _mm(a_bf16, x0) + (_mm(a_bf16, x1) + _mm(a_bf16, x2))


def _ffn_kernel(x_ref, nw_ref, wg_ref, wu_ref, wd_ref, fw_ref, o_ref, h_ref, *, final_norm):
    j = pl.program_id(1)

    @pl.when(j == 0)
    def _():
        x = x_ref[...]
        h_ref[...] = _rms(x, nw_ref[...]).astype(BF16)
        o_ref[...] = x

    h = h_ref[...]
    g = jnp.dot(h, wg_ref[...].astype(BF16), preferred_element_type=F32)
    u = jnp.dot(h, wu_ref[...].astype(BF16), preferred_element_type=F32)
    a = (0.5 * _silu(g) * u).astype(BF16)
    o_ref[...] += jnp.dot(a, wd_ref[...].astype(BF16), preferred_element_type=F32)

    if final_norm:
        @pl.when(j == pl.num_programs(1) - 1)
        def _():
            o_ref[...] = _rms(o_ref[...], fw_ref[...])


def _ffn(x, norm_w, w_gate, w_up, w_down, final_w, layer, final_norm):
    t = x.shape[0]
    tm = min(FFN_TM, t)
    grid = (t // tm, D_FF // FFN_TF)
    return pl.pallas_call(
        functools.partial(_ffn_kernel, final_norm=final_norm),
        grid=grid,
        in_specs=[
            pl.BlockSpec((tm, D_MODEL), lambda i, j: (i, 0)),
            pl.BlockSpec((None, 1, D_MODEL), lambda i, j: (layer, 0, 0)),
            pl.BlockSpec((None, D_MODEL, FFN_TF), lambda i, j: (layer, 0, j)),
            pl.BlockSpec((None, D_MODEL, FFN_TF), lambda i, j: (layer, 0, j)),
            pl.BlockSpec((None, FFN_TF, D_MODEL), lambda i, j: (layer, j, 0)),
            pl.BlockSpec((1, D_MODEL), lambda i, j: (0, 0)),
        ],
        out_specs=pl.BlockSpec((tm, D_MODEL), lambda i, j: (i, 0)),
        out_shape=jax.ShapeDtypeStruct((t, D_MODEL), F32),
        scratch_shapes=[pltpu.VMEM((tm, D_MODEL), BF16)],
        compiler_params=_cparams(("parallel", "arbitrary")),
        name="ffn",
    )(x, norm_w, w_gate, w_up, w_down, final_w)


def _wprep_kernel(w_ref, o_ref):
    j = pl.program_id(0)
    rows = j * WPREP_TN + lax.broadcasted_iota(jnp.int32, (WPREP_TN, D_MODEL), 0)
    for l in range(DEPTH):
        o_ref[l] = jnp.where(rows < P_MAIN + P_TAIL, w_ref[:, l, :], 0.0).astype(BF16)


def _wprep(w_in_t):
    return pl.pallas_call(
        _wprep_kernel,
        grid=(P_PAD // WPREP_TN,),
        in_specs=[pl.BlockSpec((WPREP_TN, DEPTH, D_MODEL), lambda j: (j, 0, 0))],
        out_specs=pl.BlockSpec((DEPTH, WPREP_TN, D_MODEL), lambda j: (0, j, 0)),
        out_shape=jax.ShapeDtypeStruct((DEPTH, P_PAD, D_MODEL), BF16),
        compiler_params=_cparams(("parallel",)),
        name="wprep",
    )(w_in_t)


def _inproj_kernel(x_ref, nw_ref, w_ref, wt_ref, p_ref, pt_ref, h_ref):
    j = pl.program_id(1)

    @pl.when(j == 0)
    def _():
        h = _rms(x_ref[...], nw_ref[...]).astype(BF16)
        h_ref[...] = h
        pt_ref[...] = _dot_nt(h, wt_ref[...])

    p_ref[...] = _dot_nt(h_ref[...], w_ref[...])


def _inproj(x, norm_w, w_in_tb, layer):
    t = x.shape[0]
    tm = min(INPROJ_TM, t)
    grid = (t // tm, P_MAIN // INPROJ_TN)
    return pl.pallas_call(
        _inproj_kernel,
        grid=grid,
        in_specs=[
            pl.BlockSpec((tm, D_MODEL), lambda i, j: (i, 0)),
            pl.BlockSpec((None, 1, D_MODEL), lambda i, j: (layer, 0, 0)),
            pl.BlockSpec((None, INPROJ_TN, D_MODEL), lambda i, j: (layer, j, 0)),
            pl.BlockSpec((None, LANES, D_MODEL), lambda i, j: (layer, P_MAIN // LANES, 0)),
        ],
        out_specs=[
            pl.BlockSpec((tm, INPROJ_TN), lambda i, j: (i, j)),
            pl.BlockSpec((tm, LANES), lambda i, j: (i, 0)),
        ],
        out_shape=[jax.ShapeDtypeStruct((t, P_MAIN), F32),
                   jax.ShapeDtypeStruct((t, LANES), F32)],
        scratch_shapes=[pltpu.VMEM((tm, D_MODEL), BF16)],
        compiler_params=_cparams(("parallel", "arbitrary")),
        name="inproj",
    )(x, norm_w, w_in_tb, w_in_tb)


def _outproj_kernel(x_ref, oa_ref, ob_ref, oc_ref, w_ref, o_ref, wb_ref):
    @pl.when(pl.program_id(0) == 0)
    def _():
        wb_ref[...] = w_ref[...].astype(BF16)

    acc = x_ref[...] + jnp.dot(oa_ref[...], wb_ref[0:A_W, :], preferred_element_type=F32)
    acc = acc + jnp.dot(ob_ref[...], wb_ref[A_W:A_W + B_W, :], preferred_element_type=F32)
    o_ref[...] = acc + jnp.dot(oc_ref[...], wb_ref[A_W + B_W:, :], preferred_element_type=F32)


def _outproj(x, oa, ob, oc, w_out, layer):
    t = x.shape[0]
    tm = min(OUTPROJ_TM, t)
    row = lambda w: pl.BlockSpec((tm, w), lambda i: (i, 0))
    return pl.pallas_call(
        _outproj_kernel,
        grid=(t // tm,),
        in_specs=[row(D_MODEL), row(A_W), row(B_W), row(C_W),
                  pl.BlockSpec((None, D_MODEL, D_MODEL), lambda i: (layer, 0, 0),
                               pipeline_mode=pl.Buffered(1))],
        out_specs=row(D_MODEL),
        out_shape=jax.ShapeDtypeStruct((t, D_MODEL), F32),
        scratch_shapes=[pltpu.VMEM((D_MODEL, D_MODEL), BF16)],
        compiler_params=_cparams(("arbitrary",)),
        name="outproj",
    )(x, oa, ob, oc, w_out)


def _rope_tables(s):
    half = A_QK_DIM // 2
    inv_freq = 1.0 / (ROPE_THETA ** (jnp.arange(half, dtype=F32) / half))
    ang = jnp.arange(s).astype(F32)[:, None] * inv_freq[None, :]
    cos, sin = jnp.cos(ang), jnp.sin(ang)
    cos_t = jnp.concatenate([cos, cos, cos, cos], axis=-1)
    sin_t = jnp.concatenate([-sin, sin, -sin, sin], axis=-1)
    return cos_t, sin_t


def _attn_prep_kernel(q_ref, k_ref, v_ref, cos_ref, sin_ref, qo_ref, k0_ref, k1_ref, vo_ref):
    cos = cos_ref[...]
    sin = sin_ref[...]
    lane = lax.broadcasted_iota(jnp.int32, cos.shape, 1)
    first_half = (lane & (A_QK_DIM - 1)) < (A_QK_DIM // 2)
    comp0 = lane < A_QK_DIM
    half = A_QK_DIM // 2

    def rope(t):
        swapped = jnp.where(first_half, pltpu.roll(t, LANES - half, 1), pltpu.roll(t, half, 1))
        return t * cos + swapped * sin

    for h in range(A_HEADS):
        sl = slice(h * HEAD_W, (h + 1) * HEAD_W)
        qo_ref[:, sl] = (rope(q_ref[:, sl]) * (A_QK_DIM ** -0.5 * LOG2_E)).astype(BF16)
        kr = rope(k_ref[:, sl])
        k0_ref[:, sl] = jnp.where(comp0, kr, 0.0).astype(BF16)
        k1_ref[:, sl] = jnp.where(comp0, 0.0, kr).astype(BF16)
    vo_ref[...] = v_ref[...].astype(BF16)


def _attn_prep(p, cos_t, sin_t, s):
    t = p.shape[0]
    ts = min(PREP_TS, s)
    ns = s // ts
    blk = lambda c: pl.BlockSpec((ts, A_W), lambda i: (i, c))
    tab = pl.BlockSpec((ts, LANES), lambda i: (i % ns, 0))
    out = jax.ShapeDtypeStruct((t, A_W), BF16)
    return pl.pallas_call(
        _attn_prep_kernel,
        grid=(t // ts,),
        in_specs=[blk(0), blk(1), blk(2), tab, tab],
        out_specs=[pl.BlockSpec((ts, A_W), lambda i: (i, 0))] * 4,
        out_shape=[out] * 4,
        compiler_params=_cparams(("parallel",)),
        name="attn_prep",
    )(p, p, p, cos_t, sin_t)


def _attn_kernel(q_ref, k0_ref, k1_ref, v_ref, lam_ref, gain_ref, o_ref,
                 m_ref, l_ref, acc_ref, s_ref, *, lambda_init, tq, tk):
    qi = pl.program_id(2)
    m_ref[...] = jnp.full(m_ref.shape, -jnp.inf, F32)
    l_ref[...] = jnp.zeros(l_ref.shape, F32)
    acc_ref[...] = jnp.zeros(acc_ref.shape, F32)
    lanes = [slice(hh * HEAD_W, (hh + 1) * HEAD_W) for hh in range(ATT_HPS)]
    chains = [(hh, k_ref) for hh in range(ATT_HPS) for k_ref in (k0_ref, k1_ref)]
    ids = range(len(chains))

    def keys(kt):
        return pl.ds(pl.multiple_of(kt * tk, tk), tk)

    def scores(kt, slot, q_lo=0):
        for c, (hh, k_ref) in enumerate(chains):
            s_ref[slot, c, :, q_lo:] = _dot_nt(k_ref[keys(kt), lanes[hh]], q_ref[q_lo:, lanes[hh]])

    def absorb(kt, slot, masked, q_lo=0, q_hi=tq):
        qs = slice(q_lo, q_hi)
        nq = q_hi - q_lo
        s = [s_ref[slot, c, :, qs] for c in ids]
        if masked:
            kidx = kt * tk + lax.broadcasted_iota(jnp.int32, (tk, nq), 0)
            qidx = qi * tq + q_lo + lax.broadcasted_iota(jnp.int32, (tk, nq), 1)
            s = [jnp.where(kidx <= qidx, s[c], -jnp.inf) for c in ids]
        m_old = [m_ref[c, :, qs] for c in ids]
        m_new = [jnp.maximum(m_old[c], jnp.max(s[c], axis=0, keepdims=True)) for c in ids]
        alpha = [jnp.exp2(m_old[c] - m_new[c]) for c in ids]
        p = [jnp.exp2(s[c] - m_new[c]) for c in ids]
        for c in ids:
            l_ref[c, :, qs] = alpha[c] * l_ref[c, :, qs] + jnp.sum(p[c], axis=0, keepdims=True)
            m_ref[c, :, qs] = m_new[c]
        pv = [_dot_tn(v_ref[keys(kt), lanes[chains[c][0]]], p[c]) for c in ids]
        for c in ids:
            acc_ref[c, :, qs] = alpha[c] * acc_ref[c, :, qs] + pv[c]

    assert tq == 2 * tk
    n_full = 2 * qi
    scores(0, 0)

    def body(kp, carry):
        kt = 2 * kp
        scores(kt + 1, 1)
        absorb(kt, 0, False)
        scores(kt + 2, 0)
        absorb(kt + 1, 1, False)
        return carry

    lax.fori_loop(0, qi, body, 0)
    scores(n_full + 1, 1, q_lo=tk)
    absorb(n_full, 0, True, 0, tk)
    absorb(n_full, 0, False, tk, tq)
    absorb(n_full + 1, 1, True, tk, tq)

    lp = lam_ref[...]
    lam = (jnp.exp(jnp.sum(lp[0:1] * lp[1:2], axis=-1, keepdims=True))
           - jnp.exp(jnp.sum(lp[2:3] * lp[3:4], axis=-1, keepdims=True)) + lambda_init)
    for hh in range(ATT_HPS):
        c0, c1 = 2 * hh, 2 * hh + 1
        o_t = acc_ref[c0] / l_ref[c0] - lam * (acc_ref[c1] / l_ref[c1])
        o_ref[:, lanes[hh]] = (_rms(o_t.T, gain_ref[...]) * (1.0 - lambda_init)).astype(BF16)


def _attn(qr, k0, k1, vb, lam_p, gain, layer, b, s):
    tq = min(ATT_TQ, s)
    tk = min(ATT_TK, tq)
    nq = s // tq
    lambda_init = 0.8 - 0.6 * math.exp(-0.3 * layer)
    qmap = lambda bi, h, qi: (bi * nq + qi, h)
    kmap = lambda bi, h, qi: (bi, h)
    hw = ATT_HPS * HEAD_W
    nchain = 2 * ATT_HPS
    assert A_HEADS % ATT_HPS == 0
    return pl.pallas_call(
        functools.partial(_attn_kernel, lambda_init=lambda_init, tq=tq, tk=tk),
        grid=(b, A_HEADS // ATT_HPS, nq),
        in_specs=[
            pl.BlockSpec((tq, hw), qmap),
            pl.BlockSpec((s, hw), kmap),
            pl.BlockSpec((s, hw), kmap),
            pl.BlockSpec((s, hw), kmap),
            pl.BlockSpec((None, 4, A_QK_DIM), lambda bi, h, qi: (layer, 0, 0)),
            pl.BlockSpec((None, 1, A_V_DIM), lambda bi, h, qi: (layer, 0, 0)),
        ],
        out_specs=pl.BlockSpec((tq, hw), qmap),
        out_shape=jax.ShapeDtypeStruct((b * s, A_W), BF16),
        scratch_shapes=[pltpu.VMEM((nchain, 1, tq), F32), pltpu.VMEM((nchain, 1, tq), F32),
                        pltpu.VMEM((nchain, A_V_DIM, tq), F32),
                        pltpu.VMEM((2, nchain, tk, tq), F32)],
        compiler_params=_cparams(("parallel", "parallel", "arbitrary")),
        name="diff_attn",
    )(qr, k0, k1, vb, lam_p, gain)


def _group_row(x, group, row):
    n = x.shape[0]
    xg = x.reshape(n // group, group, x.shape[1])
    return jnp.broadcast_to(xg[:, row:row + 1, :], xg.shape).reshape(x.shape)


def _hgrn_kernel(q_ref, f_ref, i_ref, g_ref, lbp_ref, gain_ref, o_ref, st_ref, *, layer):
    c = pl.program_id(2)
    n = HGRN_C

    @pl.when(c == 0)
    def _():
        st_ref[...] = jnp.zeros(st_ref.shape, F32)

    lp = lbp_ref[...]
    e = jnp.exp(lp - jnp.max(lp, axis=0, keepdims=True))
    sm = e / jnp.sum(e, axis=0, keepdims=True)
    lb_all = jnp.zeros((1, HGRN_HPS * HEAD_W), F32)
    for r in range(1, layer + 1):
        lb_all = lb_all + sm[r:r + 1]

    row = lax.broadcasted_iota(jnp.int32, (n, n), 0)
    col = lax.broadcasted_iota(jnp.int32, (n, n), 1)
    tok = lax.broadcasted_iota(jnp.int32, (n, HEAD_W), 0)
    tril = (row >= col).astype(BF16)
    gain = gain_ref[...]
    lanes = [slice(hh * HEAD_W, (hh + 1) * HEAD_W) for hh in range(HGRN_HPS)]

    hs = []
    for hh, ln in enumerate(lanes):
        lb = lb_all[:, ln]
        fg = lb + (1.0 - lb) * jax.nn.sigmoid(f_ref[:, ln])
        hs.append(dict(fg=fg, kk=1.0 - fg, qs=_silu(q_ref[:, ln]), v=i_ref[:, ln],
                       st=st_ref[hh]))
    for h in hs:
        h["b"] = _dot_left01(tril, jnp.log2(h["fg"]))
    for h in hs:
        h["o"] = _dot_nt(h["qs"] * jnp.exp2(h["b"]), h["st"])
        h["a"] = jnp.zeros((n, n), F32)

    half = n // 2
    while half >= HGRN_SUB:
        upper = (tok & half) != 0
        same = (row & -(2 * half)) == (col & -(2 * half))
        for h in hs:
            d = h["b"] - _group_row(h["b"], 2 * half, half - 1)
            e = jnp.exp2(jnp.where(upper, d, -d))
            qh = jnp.where(upper, h["qs"] * e, 0.0)
            kh = jnp.where(upper, 0.0, h["kk"] * e)
            h["a"] = h["a"] + jnp.where(same, _dot_nt(qh, kh), 0.0)
        half //= 2
    for h in hs:
        h["o"] = h["o"] + _dot(h["a"], h["v"])

    tmod = tok & (HGRN_SUB - 1)
    for h in hs:
        h["o"] = h["o"] + jnp.sum(h["qs"] * h["kk"], axis=-1, keepdims=True) * h["v"]
    for r in range(1, HGRN_SUB):
        valid = tmod >= r
        for h in hs:
            kr, br, vr = (pltpu.roll(h[name], r, 0) for name in ("kk", "b", "v"))
            z = jnp.where(valid, h["qs"] * kr * jnp.exp2(jnp.where(valid, h["b"] - br, 0.0)), 0.0)
            h["o"] = h["o"] + jnp.sum(z, axis=-1, keepdims=True) * vr

    for hh, (h, ln) in enumerate(zip(hs, lanes)):
        o_ref[:, ln] = (_rms(h["o"], gain) * _silu(g_ref[:, ln])).astype(BF16)
        b_last = h["b"][n - 1:n, :]
        kd = h["kk"] * jnp.exp2(b_last - h["b"])
        st_ref[hh] = h["st"] * jnp.exp2(b_last) + _dot_tn(h["v"], kd)


def _hgrn(p, lb_param, gain, layer, b, s):
    nc = s // HGRN_C
    hw = HGRN_HPS * HEAD_W
    ngrp = B_HEADS // HGRN_HPS
    assert B_HEADS % HGRN_HPS == 0 and (3 * A_W) % hw == 0
    col0 = 3 * A_W // hw
    blk = lambda off: pl.BlockSpec((HGRN_C, hw),
                                   lambda bi, h, c: (bi * nc + c, col0 + off * ngrp + h))
    return pl.pallas_call(
        functools.partial(_hgrn_kernel, layer=layer),
        grid=(b, ngrp, nc),
        in_specs=[blk(0), blk(1), blk(2), blk(3),
                  pl.BlockSpec((DEPTH, hw), lambda bi, h, c: (0, h)),
                  pl.BlockSpec((None, 1, HEAD_W), lambda bi, h, c: (layer, 0, 0))],
        out_specs=pl.BlockSpec((HGRN_C, hw), lambda bi, h, c: (bi * nc + c, h)),
        out_shape=jax.ShapeDtypeStruct((b * s, B_W), BF16),
        scratch_shapes=[pltpu.VMEM((HGRN_HPS, HEAD_W, HEAD_W), F32)],
        compiler_params=_cparams(("parallel", "parallel", "arbitrary")),
        name="hgrn2",
    )(p, p, p, p, lb_param, gain)


def _lane_pick(x, idx):
    lane = lax.broadcasted_iota(jnp.int32, x.shape, 1)
    return jnp.sum(jnp.where(lane == idx, x, 0.0), axis=-1, keepdims=True)


def _gdn_kernel(q_ref, k_ref, v_ref, z_ref, pt_ref, cq_ref, ck_ref, cv_ref, alog_ref, dtb_ref,
                gain_ref, o_ref, xe_ref, s_ref):
    hg = pl.program_id(1)
    c = pl.program_id(2)
    nb, cs, cb = GDN_NB, GDN_C, GDN_CB
    halo = SUBLANES

    @pl.when(c == 0)
    def _():
        s_ref[...] = jnp.zeros(s_ref.shape, F32)
        xe_ref[:, 0:halo, :] = jnp.zeros((3 * GDN_HPS, halo, HEAD_W), F32)

    row = lax.broadcasted_iota(jnp.int32, (cb, cb), 0)
    col = lax.broadcasted_iota(jnp.int32, (cb, cb), 1)
    same = (row & -cs) == (col & -cs)
    incl = jnp.logical_and(same, row >= col)
    strict = jnp.logical_and(same, row > col)
    eye = (row == col).astype(F32)
    tril01 = incl.astype(BF16)
    pt = pt_ref[...]
    alog_row = alog_ref[...]
    dtb_row = dtb_ref[...]
    gain = gain_ref[...]

    def l2n(x):
        return x * lax.rsqrt(jnp.sum(x * x, axis=-1, keepdims=True) + EPS)

    heads = range(GDN_HPS)
    lanes = [slice(hh * HEAD_W, (hh + 1) * HEAD_W) for hh in heads]

    def conv(hh, idx, x_ref, w_ref):
        slot = 3 * hh + idx
        xe_ref[slot, halo:halo + cb, :] = x_ref[:, lanes[hh]]
        w = w_ref[:, lanes[hh]]
        y = jnp.zeros((cb, HEAD_W), F32)
        for j in range(CONV_K):
            off = halo - (CONV_K - 1) + j
            y = y + w[j:j + 1, :] * xe_ref[slot, off:off + cb, :]
        xe_ref[slot, 0:halo, :] = xe_ref[slot, cb:cb + halo, :]
        return _silu(y)

    def st_inputs(h):
        hh = h["hh"]
        hd = hg * GDN_HPS + hh
        h["q"] = l2n(conv(hh, 0, q_ref, cq_ref)) * (HEAD_W ** -0.5)
        h["k"] = l2n(conv(hh, 1, k_ref, ck_ref))
        v = conv(hh, 2, v_ref, cv_ref)
        beta = jax.nn.sigmoid(_lane_pick(pt, hd))
        xg = _lane_pick(pt, C_HEADS + hd) + _lane_pick(dtb_row, hd)
        softplus = jnp.maximum(xg, 0.0) + jnp.log(1.0 + jnp.exp(-jnp.abs(xg)))
        h["g"] = (-LOG2_E) * jnp.exp(_lane_pick(alog_row, hd)) * softplus
        h["kb"] = h["k"] * beta
        h["vb"] = v * beta

    def st_cumdecay(h):
        h["bc"] = _dot_left01(tril01, jnp.broadcast_to(h["g"], (cb, HEAD_W)))

    def st_decay(h):
        bc2 = jnp.concatenate([h["bc"], h["bc"]], axis=1)
        bc_row = h["bc"].T[0:1, :]
        h["decay"] = jnp.where(incl, jnp.exp2(jnp.where(incl, bc2 - bc_row, 0.0)), 0.0)
        h["ebc"] = jnp.exp2(h["bc"])

    def st_scores(h):
        kq = _dot_nt(jnp.concatenate([h["kb"], h["q"]], axis=0), h["k"])
        h["lmat"] = jnp.where(strict, kq[:cb] * h["decay"], 0.0)
        h["attn"] = kq[cb:] * h["decay"]

    def st_square(h):
        h["x"] = eye - h["lmat"]
        lh, ll = _split(h["lmat"])
        h["pw"] = _dot2s(lh, ll, lh)

    def st_level(last):
        def run(h):
            xh, xl = _split(h["x"])
            ph, pl_ = _split(h["pw"])
            if last:
                h["x"] = h["x"] + _dot2s(xh, xl, ph)
            else:
                both = _dot2s(jnp.concatenate([xh, ph], axis=0),
                              jnp.concatenate([xl, pl_], axis=0), ph)
                h["x"] = h["x"] + both[:cb]
                h["pw"] = both[cb:]
        return run

    def st_solve(h):
        rhs = jnp.concatenate([h["vb"], h["kb"] * h["ebc"]], axis=1)
        uw = _dot2s(*_split(h["x"]), rhs.astype(BF16))
        h["u"] = uw[:, :HEAD_W]
        h["w"] = uw[:, HEAD_W:]
        h["qe"] = h["q"] * h["ebc"]
        h["v_news"], h["o_inter"] = [], []
        h["s"] = s_ref[h["hh"]]

    def st_chunk(n):
        def run(h):
            sl = slice(n * cs, (n + 1) * cs)
            bc = h["bc"]
            b_last = bc[(n + 1) * cs - 1:(n + 1) * cs, :]
            wq = _dot(jnp.concatenate([h["w"][sl], h["qe"][sl]], axis=0), h["s"])
            v_new = h["u"][sl] - wq[:cs]
            h["o_inter"].append(wq[cs:])
            h["v_news"].append(v_new)
            h["s"] = (h["s"] * jnp.exp2(b_last)
                      + _dot_tn(h["k"][sl] * jnp.exp2(b_last - bc[sl]), v_new))
        return run

    def st_output(h):
        hh = h["hh"]
        s_ref[hh] = h["s"]
        o = (jnp.concatenate(h["o_inter"], axis=0)
             + _dot(h["attn"], jnp.concatenate(h["v_news"], axis=0)))
        o_ref[:, lanes[hh]] = (_rms(o, gain) * _silu(z_ref[:, lanes[hh]])).astype(BF16)

    n_levels = cs.bit_length() - 2
    stages = ([st_inputs, st_cumdecay, st_decay, st_scores, st_square]
              + [st_level(i == n_levels - 1) for i in range(n_levels)]
              + [st_solve] + [st_chunk(n) for n in range(nb)] + [st_output])
    hs = [dict(hh=hh) for hh in heads]
    for stage in stages:
        for h in hs:
            stage(h)


def _gdn(p, p_tail, conv_w, a_log, dt_bias, gain, layer, b, s):
    cb = GDN_CB
    assert cb == 2 * HEAD_W and s % cb == 0 and C_HEADS % GDN_HPS == 0
    nc = s // cb
    hw = GDN_HPS * HEAD_W
    col0 = (3 * A_W + 4 * B_W) // hw
    blk = lambda off: pl.BlockSpec(
        (cb, hw), lambda bi, h, c: (bi * nc + c, col0 + off * (C_HEADS // GDN_HPS) + h))
    cw = lambda off: pl.BlockSpec(
        (None, CONV_K, hw), lambda bi, h, c: (layer, 0, off * (C_HEADS // GDN_HPS) + h))
    prow = pl.BlockSpec((None, 1, LANES), lambda bi, h, c: (layer, 0, 0))
    return pl.pallas_call(
        _gdn_kernel,
        grid=(b, C_HEADS // GDN_HPS, nc),
        in_specs=[blk(0), blk(1), blk(2), blk(3),
                  pl.BlockSpec((cb, LANES), lambda bi, h, c: (bi * nc + c, 0)),
                  cw(0), cw(1), cw(2), prow, prow,
                  pl.BlockSpec((None, 1, HEAD_W), lambda bi, h, c: (layer, 0, 0))],
        out_specs=pl.BlockSpec((cb, hw), lambda bi, h, c: (bi * nc + c, h)),
        out_shape=jax.ShapeDtypeStruct((b * s, C_W), BF16),
        scratch_shapes=[pltpu.VMEM((3 * GDN_HPS, SUBLANES + cb, HEAD_W), F32),
                        pltpu.VMEM((GDN_HPS, HEAD_W, HEAD_W), F32)],
        compiler_params=_cparams(("parallel", "parallel", "arbitrary")),
        name="gdn",
    )(p, p, p, p, p_tail, conv_w, conv_w, conv_w, a_log, dt_bias, gain)


def kernel(x, ffn1_norm, ffn1_w_gate, ffn1_w_up, ffn1_w_down, mix_norm, w_in, w_out, lambda_q1, lambda_k1, lambda_q2, lambda_k2, diff_gain, hgrn_lb_param, hgrn_gain, gdn_conv_w, gdn_a_log, gdn_dt_bias, gdn_gain, ffn2_norm, ffn2_w_gate, ffn2_w_up, ffn2_w_down, final_norm):
    b, s, d = x.shape
    depth = w_in.shape[0]
    xt = x.reshape(b * s, d)

    w_in_tb = _wprep(jnp.transpose(w_in, (2, 0, 1)))
    lam_p = jnp.stack([lambda_q1, lambda_k1, lambda_q2, lambda_k2], axis=1)
    rows = lambda t: t[:, None, :]
    pad_heads = lambda t: rows(jnp.pad(t, ((0, 0), (0, LANES - t.shape[1]))))
    a_log_p = pad_heads(gdn_a_log)
    dt_bias_p = pad_heads(gdn_dt_bias)
    ffn1_norm, mix_norm, ffn2_norm = rows(ffn1_norm), rows(mix_norm), rows(ffn2_norm)
    diff_g, hgrn_g, gdn_g = rows(diff_gain), rows(hgrn_gain), rows(gdn_gain)
    final_w = final_norm.reshape(1, d)
    cos_t, sin_t = _rope_tables(s)

    for l in range(depth):
        xt = _ffn(xt, ffn1_norm, ffn1_w_gate, ffn1_w_up, ffn1_w_down, final_w, l, False)
        p, p_tail = _inproj(xt, mix_norm, w_in_tb, l)
        qr, k0, k1, vb = _attn_prep(p, cos_t, sin_t, s)
        oa = _attn(qr, k0, k1, vb, lam_p, diff_g, l, b, s)
        ob = _hgrn(p, hgrn_lb_param, hgrn_g, l, b, s)
        oc = _gdn(p, p_tail, gdn_conv_w, a_log_p, dt_bias_p, gdn_g, l, b, s)
        xt = _outproj(xt, oa, ob, oc, w_out, l)
        xt = _ffn(xt, ffn2_norm, ffn2_w_gate, ffn2_w_up, ffn2_w_down, final_w, l, l == depth - 1)
    return xt.reshape(b, s, d)
```
